```python
import jax, jax.numpy as jnp
from jax import lax
import numpy as np

D_MODEL = 1024
BATCH = 4
SEQ = 8192
DEPTH = 2

GRID_W = 64
CTX_LEN = 256
N_MIXERS = 2
CONV_WIDTH = 31
CONV_PAD = (CONV_WIDTH - 1) // 2
HG_HEAD_DIM = 128
HG_HEADS = D_MODEL // HG_HEAD_DIM
CHUNK = 64
N_EXPERTS = 16
CAPACITY_FACTOR = 2
D_EXPERT = 2 * D_MODEL
N_CONV_LAYERS = (DEPTH + 1) // 2
N_HGRN_LAYERS = DEPTH // 2
EPS = 1e-6

kernel_name = 'hybrid_conformer_hgrn2_ecmoe_prefix_dit'


def rms_norm(x, g):
    xf = x.astype(jnp.float32)
    y = xf * lax.rsqrt(jnp.mean(xf * xf, axis=-1, keepdims=True) + EPS)
    return (y * g.astype(jnp.float32)).astype(x.dtype)


def layer_norm(x, g, b):
    xf = x.astype(jnp.float32)
    mu = jnp.mean(xf, axis=-1, keepdims=True)
    xc = xf - mu
    y = xc * lax.rsqrt(jnp.mean(xc * xc, axis=-1, keepdims=True) + EPS)
    return (y * g.astype(jnp.float32) + b.astype(jnp.float32)).astype(x.dtype)


def modulate(h, shift, scale):
    return h * (1.0 + scale) + shift


def conformer_conv(h, w1, b1, dw, dwb, ln_g, ln_b, w2, b2):
    a, gate = jnp.split(h @ w1 + b1, 2, axis=-1)
    u = a * jax.nn.sigmoid(gate)
    u = lax.conv_general_dilated(u, dw[:, None, :], window_strides=(1,),
                                 padding=[(CONV_PAD, CONV_PAD)],
                                 dimension_numbers=('NWC', 'WIO', 'NWC'),
                                 feature_group_count=u.shape[-1]) + dwb
    u = jax.nn.silu(layer_norm(u, ln_g, ln_b))
    return u @ w2 + b2


def gla_chunk_scan(q, k, v, logf, s0):
    b, h, t, _ = q.shape
    nc = t // CHUNK

    def to_chunks(a):
        return jnp.moveaxis(a.reshape(b, h, nc, CHUNK, a.shape[-1]), 2, 0)

    mask = jnp.tril(jnp.ones((CHUNK, CHUNK), dtype=bool))[:, :, None]

    def step(s, xs):
        qc, kc, vc, gc = xs
        G = jnp.cumsum(gc, axis=2)
        o_inter = jnp.einsum('bhck,bhkv->bhcv', qc * jnp.exp(G), s)
        diff = G[:, :, :, None, :] - G[:, :, None, :, :]
        decay = jnp.exp(jnp.where(mask, diff, -jnp.inf))
        scores = jnp.einsum('bhtk,bhtsk,bhsk->bhts', qc, decay, kc)
        o_intra = jnp.einsum('bhts,bhsv->bhtv', scores, vc)
        g_last = G[:, :, -1:, :]
        s_new = (jnp.exp(g_last[:, :, 0, :])[..., None] * s
                 + jnp.einsum('bhsk,bhsv->bhkv', kc * jnp.exp(g_last - G), vc))
        return s_new, o_inter + o_intra

    s_fin, o = lax.scan(step, s0, (to_chunks(q), to_chunks(k), to_chunks(v), to_chunks(logf)))
    o = jnp.moveaxis(o, 0, 2).reshape(b, h, t, v.shape[-1])
    return o, s_fin


def hgrn2_mixer(h_ctx, h_lat, w_in, lb, norm_g, w_out, need_ctx):
    def heads(a):
        bsz, t, _ = a.shape
        return a.reshape(bsz, t, HG_HEADS, HG_HEAD_DIM).transpose(0, 2, 1, 3).astype(jnp.float32)

    def project(h):
        q, f_fw, f_bw, inp, g = jnp.split(h @ w_in, 5, axis=-1)
        dirs = []
        for f, lbd in ((f_fw, lb[0]), (f_bw, lb[1])):
            fg = lbd + (1.0 - lbd) * jax.nn.sigmoid(f.astype(jnp.float32))
            dirs.append((heads(1.0 - fg), heads(jnp.log(fg))))
        return heads(q), heads(inp), dirs, g

    def readout(o, g, dtype):
        bsz, _, t, _ = o.shape
        o = o.transpose(0, 2, 1, 3)
        o = o * lax.rsqrt(jnp.mean(o * o, axis=-1, keepdims=True) + EPS) * norm_g.astype(jnp.float32)
        gh = g.reshape(bsz, t, HG_HEADS, HG_HEAD_DIM).astype(jnp.float32)
        y = (o * jax.nn.silu(gh)).reshape(bsz, t, D_MODEL).astype(dtype)
        return y @ w_out

    qc, ic, dc, gc = project(h_ctx)
    ql, il, dl, gl = project(h_lat)
    s0 = jnp.zeros((h_lat.shape[0], HG_HEADS, HG_HEAD_DIM, HG_HEAD_DIM), jnp.float32)
    outs_ctx, outs_lat = [], []
    for d in range(2):
        fl = (lambda a: jnp.flip(a, axis=2)) if d == 1 else (lambda a: a)
        (kc, lfc), (kl, lfl) = dc[d], dl[d]
        oc, s_ctx = gla_chunk_scan(fl(qc), fl(kc), fl(ic), fl(lfc), s0)
        ol, _ = gla_chunk_scan(fl(ql), fl(kl), fl(il), fl(lfl), s_ctx)
        outs_ctx.append(fl(oc))
        outs_lat.append(fl(ol))
    y_lat = readout(outs_lat[0] + outs_lat[1], gl, h_lat.dtype)
    y_ctx = readout(outs_ctx[0] + outs_ctx[1], gc, h_ctx.dtype) if need_ctx else None
    return y_ctx, y_lat


def expert_choice_moe(h, router_w, w_gate, w_up, w_down):
    bsz, t, d = h.shape
    cap = CAPACITY_FACTOR * t // N_EXPERTS
    affin = jax.nn.softmax((h @ router_w).astype(jnp.float32), axis=-1)
    gate, idx = lax.top_k(jnp.swapaxes(affin, 1, 2), cap)
    xs = jax.vmap(lambda hb, ib: hb[ib])(h, idx)
    a = jnp.einsum('becd,edf->becf', xs, w_gate)
    u = jnp.einsum('becd,edf->becf', xs, w_up)
    y = jnp.einsum('becf,efd->becd', jax.nn.silu(a) * u, w_down) * gate[..., None].astype(h.dtype)
    return jax.vmap(lambda yb, ib: jnp.zeros((t, d), yb.dtype).at[ib.reshape(-1)].add(yb.reshape(-1, d)))(y, idx)


def setup_inputs(seed: int = 0) -> dict:
    key = jax.random.key(seed)
    ks = jax.random.split(key, 25)
    D, E, F = D_MODEL, N_EXPERTS, D_EXPERT

    def nrm(k, shape, s):
        return jax.random.normal(k, shape, jnp.float32) * s

    return {
        'x': nrm(ks[0], (BATCH, SEQ, D), 1.0),
        'c': nrm(ks[1], (BATCH, D), 1.0),
        'ctx': nrm(ks[2], (BATCH, CTX_LEN, D), 1.0),
        'c_ctx': nrm(ks[3], (D,), 1.0),
        'mod_w': nrm(ks[4], (DEPTH, D, 6 * D), D ** -0.5),
        'mod_b': nrm(ks[5], (DEPTH, 6 * D), 0.02),
        'norm_mix_g': 1.0 + nrm(ks[6], (DEPTH, D), 0.05),
        'norm_ffn_g': 1.0 + nrm(ks[7], (DEPTH, D), 0.05),
        'cv_w1': nrm(ks[8], (N_CONV_LAYERS, D, 2 * D), D ** -0.5),
        'cv_b1': nrm(ks[9], (N_CONV_LAYERS, 2 * D), 0.02),
        'cv_dw': nrm(ks[10], (N_CONV_LAYERS, CONV_WIDTH, D), CONV_WIDTH ** -0.5),
        'cv_dwb': nrm(ks[11], (N_CONV_LAYERS, D), 0.02),
        'cv_ln_g': 1.0 + nrm(ks[12], (N_CONV_LAYERS, D), 0.05),
        'cv_ln_b': nrm(ks[13], (N_CONV_LAYERS, D), 0.02),
        'cv_w2': nrm(ks[14], (N_CONV_LAYERS, D, D), D ** -0.5),
        'cv_b2': nrm(ks[15], (N_CONV_LAYERS, D), 0.02),
        'hg_w_in': nrm(ks[16], (N_HGRN_LAYERS, D, 5 * D), D ** -0.5),
        'hg_lb_logits': nrm(ks[17], (DEPTH, 2, D), 0.5),
        'hg_norm_g': 1.0 + nrm(ks[18], (N_HGRN_LAYERS, HG_HEAD_DIM), 0.05),
        'hg_w_out': nrm(ks[19], (N_HGRN_LAYERS, D, D), D ** -0.5),
        'router_w': nrm(ks[20], (DEPTH, D, E), D ** -0.5),
        'exp_w_gate': nrm(ks[21], (DEPTH, E, D, F), D ** -0.5),
        'exp_w_up': nrm(ks[22], (DEPTH, E, D, F), D ** -0.5),
        'exp_w_down': nrm(ks[23], (DEPTH, E, F, D), F ** -0.5),
        'final_g': 1.0 + nrm(ks[24], (D,), 0.05),
    }


def reference(x, c, ctx, c_ctx, mod_w, mod_b, norm_mix_g, norm_ffn_g, cv_w1, cv_b1, cv_dw, cv_dwb,
              cv_ln_g, cv_ln_b, cv_w2, cv_b2, hg_w_in, hg_lb_logits, hg_norm_g, hg_w_out,
              router_w, exp_w_gate, exp_w_up, exp_w_down, final_g):
    lat = x
    lb_soft = jax.nn.softmax(hg_lb_logits.astype(jnp.float32), axis=0)
    lb_all = jnp.cumsum(lb_soft, axis=0) - lb_soft[0]
    cond_lat = jax.nn.silu(c)
    cond_ctx = jax.nn.silu(c_ctx)
    for i in range(DEPTH):
        last = i == DEPTH - 1
        j = i // N_MIXERS
        m_lat = jnp.split((cond_lat @ mod_w[i] + mod_b[i])[:, None, :], 6, axis=-1)
        m_ctx = jnp.split((cond_ctx @ mod_w[i] + mod_b[i])[None, None, :], 6, axis=-1)

        h_lat = modulate(rms_norm(lat, norm_mix_g[i]), m_lat[0], m_lat[1])
        h_ctx = modulate(rms_norm(ctx, norm_mix_g[i]), m_ctx[0], m_ctx[1])
        if i % N_MIXERS == 0:
            cp = (cv_w1[j], cv_b1[j], cv_dw[j], cv_dwb[j], cv_ln_g[j], cv_ln_b[j], cv_w2[j], cv_b2[j])
            y_lat = conformer_conv(h_lat, *cp)
            y_ctx = None if last else conformer_conv(h_ctx, *cp)
        else:
            y_ctx, y_lat = hgrn2_mixer(h_ctx, h_lat, hg_w_in[j], lb_all[i], hg_norm_g[j], hg_w_out[j],
                                       not last)
        lat = lat + m_lat[2] * y_lat
        if not last:
            ctx = ctx + m_ctx[2] * y_ctx

        h_lat = modulate(rms_norm(lat, norm_ffn_g[i]), m_lat[3], m_lat[4])
        lat = lat + m_lat[5] * expert_choice_moe(h_lat, router_w[i], exp_w_gate[i], exp_w_up[i], exp_w_down[i])
        if not last:
            h_ctx = modulate(rms_norm(ctx, norm_ffn_g[i]), m_ctx[3], m_ctx[4])
            ctx = ctx + m_ctx[5] * expert_choice_moe(h_ctx, router_w[i], exp_w_gate[i], exp_w_up[i], exp_w_down[i])
    return rms_norm(lat, final_g)
```

```python
import functools

import jax
import jax.numpy as jnp
from jax import lax
from jax.experimental import pallas as pl
from jax.experimental.pallas import tpu as pltpu

F32 = jnp.float32
BF16 = jnp.bfloat16
I32 = jnp.int32
EPS = 1e-6
HIGHEST = lax.Precision.HIGHEST

LANES = 128
HEAD = 128
CHUNK = 64
CAPACITY_FACTOR = 2
HALO = 16
WIN = 64
ROW_ALIGN = 16
VMEM_LIMIT_BYTES = 56 * 1024 * 1024


def _cparams(sem):
    return pltpu.CompilerParams(dimension_semantics=sem, vmem_limit_bytes=VMEM_LIMIT_BYTES)


def _silu(x):
    return x * jax.nn.sigmoid(x)


def _rms(x, g):
    ms = jnp.mean(x * x, axis=-1, keepdims=True)
    return x * lax.rsqrt(ms + EPS) * g


def _prenorm(x, g, shift, scale):
    return _rms(x, g) * (1.0 + scale) + shift


def _mod_body(c_ref, w_ref, b_ref, o_ref):
    c = c_ref[...]
    o_ref[0] = jnp.dot(_silu(c), w_ref[0], precision=HIGHEST,
                       preferred_element_type=F32) + b_ref[0]


def _mod(cond8, mod_w, mod_b):
    depth, d, d6 = mod_w.shape
    return pl.pallas_call(
        _mod_body,
        grid=(depth, d6 // d),
        in_specs=[pl.BlockSpec((8, d), lambda l, j: (0, 0)),
                  pl.BlockSpec((1, d, d), lambda l, j: (l, 0, j)),
                  pl.BlockSpec((1, 1, d), lambda l, j: (l, 0, j))],
        out_specs=pl.BlockSpec((1, 8, d), lambda l, j: (l, 0, j)),
        out_shape=jax.ShapeDtypeStruct((depth, 8, d6), F32),
        compiler_params=_cparams(("arbitrary", "arbitrary")),
        name="mod",
    )(cond8, mod_w, mod_b.reshape(depth, 1, d6))


def _post(lat, y, m, gffn, rw_ref, n_exp, lat_o, hx_o, aff_o):
    d = lat.shape[-1]
    lat_new = lat + m[2:3] * y
    h = _prenorm(lat_new, gffn, m[3:4], m[4:5])
    logits = jnp.dot(h, rw_ref[...], precision=HIGHEST, preferred_element_type=F32)
    lane = lax.broadcasted_iota(I32, logits.shape, 1)
    lg = jnp.where(lane < n_exp, logits, -jnp.inf)
    ex = jnp.exp(lg - jnp.max(lg, axis=-1, keepdims=True))
    aff = ex / jnp.sum(ex, axis=-1, keepdims=True)
    lat_o[0] = lat_new
    hx_o[0, :, :d] = h
    hx_o[0, :, d:] = aff
    aff_o[0] = aff.T[:n_exp]


def _glu_body(x_ref, m_ref, g_ref, w1_ref, b1_ref, u_ref):
    d = x_ref.shape[-1]
    m = m_ref[0]
    h = _prenorm(x_ref[0], g_ref[...], m[0:1], m[1:2]).astype(BF16)
    a = jnp.dot(h, w1_ref[:, :d], preferred_element_type=F32) + b1_ref[:, :d]
    gate = jnp.dot(h, w1_ref[:, d:], preferred_element_type=F32) + b1_ref[:, d:]
    u_ref[0] = (a * jax.nn.sigmoid(gate)).astype(u_ref.dtype)


def _glu(x, m, mrow, g, w1, b1, tt):
    bn, t, d = x.shape
    return pl.pallas_call(
        _glu_body,
        grid=(bn, t // tt),
        in_specs=[pl.BlockSpec((1, tt, d), lambda b, i: (b, i, 0)),
                  pl.BlockSpec((1, 6, d), lambda b, i: (mrow(b), 0, 0)),
                  pl.BlockSpec((1, d), lambda b, i: (0, 0)),
                  pl.BlockSpec((d, 2 * d), lambda b, i: (0, 0)),
                  pl.BlockSpec((1, 2 * d), lambda b, i: (0, 0))],
        out_specs=pl.BlockSpec((1, tt, d), lambda b, i: (b, i, 0)),
        out_shape=jax.ShapeDtypeStruct((bn, t, d), BF16),
        compiler_params=_cparams(("arbitrary", "arbitrary")),
        name="glu",
    )(x, m, g, w1, b1)


def _conv_body(nt, kw, n_exp, up_ref, uc_ref, un_ref, x_ref, m_ref, dw_ref, dwb_ref, lng_ref,
               lnb_ref, w2_ref, b2_ref, gffn_ref, rw_ref, lat_o, hx_o, aff_o, xs_ref, cv_ref):
    i = pl.program_id(1)
    tt, d = cv_ref.shape
    pad = (kw - 1) // 2
    xs_ref[0:HALO, :] = jnp.where(i > 0, up_ref[0].astype(F32), 0.0)
    xs_ref[HALO:HALO + tt, :] = uc_ref[0].astype(F32)
    xs_ref[HALO + tt:, :] = jnp.where(i < nt - 1, un_ref[0].astype(F32), 0.0)

    strip = min(tt, 128)

    def chan_block(j, carry):
        c0 = pl.multiple_of(j * LANES, LANES)
        for s in range(tt // strip):
            acc = jnp.zeros((strip, LANES), F32)
            for k in range(kw):
                off = s * strip + k + HALO - pad
                acc = acc + xs_ref[off:off + strip, pl.ds(c0, LANES)] * dw_ref[k:k + 1, pl.ds(c0, LANES)]
            cv_ref[s * strip:(s + 1) * strip, pl.ds(c0, LANES)] = acc + dwb_ref[:, pl.ds(c0, LANES)]
        return carry

    lax.fori_loop(0, d // LANES, chan_block, 0)

    cv = cv_ref[...]
    mu = jnp.mean(cv, axis=-1, keepdims=True)
    xc = cv - mu
    var = jnp.mean(xc * xc, axis=-1, keepdims=True)
    y = xc * lax.rsqrt(var + EPS) * lng_ref[...] + lnb_ref[...]
    z = jnp.dot(_silu(y).astype(BF16), w2_ref[...], preferred_element_type=F32) + b2_ref[...]
    _post(x_ref[0], z, m_ref[0], gffn_ref[...], rw_ref, n_exp, lat_o, hx_o, aff_o)


def _conv(u, x, m, mrow, dw, dwb, lng, lnb, w2, b2, gffn, rw_pad, n_exp, tt):
    bn, t, d = x.shape
    nt = t // tt
    kw = dw.shape[0]
    hb = tt // HALO
    nhb = t // HALO
    row = lambda b, i: (0, 0)
    return pl.pallas_call(
        functools.partial(_conv_body, nt, kw, n_exp),
        grid=(bn, nt),
        in_specs=[pl.BlockSpec((1, HALO, d), lambda b, i: (b, jnp.maximum(i * hb - 1, 0), 0)),
                  pl.BlockSpec((1, tt, d), lambda b, i: (b, i, 0)),
                  pl.BlockSpec((1, HALO, d), lambda b, i: (b, jnp.minimum((i + 1) * hb, nhb - 1), 0)),
                  pl.BlockSpec((1, tt, d), lambda b, i: (b, i, 0)),
                  pl.BlockSpec((1, 6, d), lambda b, i: (mrow(b), 0, 0)),
                  pl.BlockSpec((kw, d), row),
                  pl.BlockSpec((1, d), row), pl.BlockSpec((1, d), row), pl.BlockSpec((1, d), row),
                  pl.BlockSpec((d, d), row), pl.BlockSpec((1, d), row), pl.BlockSpec((1, d), row),
                  pl.BlockSpec((d, LANES), row)],
        out_specs=[pl.BlockSpec((1, tt, d), lambda b, i: (b, i, 0)),
                   pl.BlockSpec((1, tt, d + LANES), lambda b, i: (b, i, 0)),
                   pl.BlockSpec((1, n_exp, tt), lambda b, i: (b, 0, i))],
        out_shape=[jax.ShapeDtypeStruct((bn, t, d), F32),
                   jax.ShapeDtypeStruct((bn, t, d + LANES), F32),
                   jax.ShapeDtypeStruct((bn, n_exp, t), F32)],
        scratch_shapes=[pltpu.VMEM((tt + 2 * HALO, d), F32), pltpu.VMEM((tt, d), F32)],
        compiler_params=_cparams(("arbitrary", "arbitrary")),
        name="conv",
    )(u, u, u, x, m, dw, dwb, lng, lnb, w2, b2, gffn, rw_pad)


def _proj_body(layer, x_ref, m_ref, g_ref, w_ref, lbl_ref, q_o, kf_o, kb_o, lf_o, lb_o, v_o, g_o):
    d = x_ref.shape[-1]
    m = m_ref[0]
    h = _prenorm(x_ref[0], g_ref[...], m[0:1], m[1:2]).astype(BF16)
    lg = lbl_ref[...]
    ex = jnp.exp(lg - jnp.max(lg, axis=0, keepdims=True))
    soft = ex / jnp.sum(ex, axis=0, keepdims=True)
    lbd = jnp.zeros(soft.shape[1:], F32)
    for l in range(1, layer + 1):
        lbd = lbd + soft[l]

    def col(j):
        return jnp.dot(h, w_ref[:, j * d:(j + 1) * d], preferred_element_type=F32)

    q_o[0] = col(0).astype(q_o.dtype)
    for dr, (k_o, l_o) in enumerate(((kf_o, lf_o), (kb_o, lb_o))):
        lb_row = lbd[dr:dr + 1]
        fg = lb_row + (1.0 - lb_row) * jax.nn.sigmoid(col(1 + dr))
        k_o[0] = (1.0 - fg).astype(k_o.dtype)
        l_o[0] = jnp.log(fg)
    v_o[0] = col(3).astype(v_o.dtype)
    g_o[0] = col(4).astype(g_o.dtype)


def _proj(x, m, mrow, g, w_in, lb_logits, layer, tt):
    bn, t, d = x.shape
    blk = pl.BlockSpec((1, tt, d), lambda b, i: (b, i, 0))
    sd = lambda dt: jax.ShapeDtypeStruct((bn, t, d), dt)
    return pl.pallas_call(
        functools.partial(_proj_body, layer),
        grid=(bn, t // tt),
        in_specs=[blk,
                  pl.BlockSpec((1, 6, d), lambda b, i: (mrow(b), 0, 0)),
                  pl.BlockSpec((1, d), lambda b, i: (0, 0)),
                  pl.BlockSpec((d, 5 * d), lambda b, i: (0, 0)),
                  pl.BlockSpec(lb_logits.shape, lambda b, i: (0, 0, 0))],
        out_specs=[blk] * 7,
        out_shape=[sd(BF16), sd(BF16), sd(BF16), sd(F32), sd(F32), sd(BF16), sd(BF16)],
        compiler_params=_cparams(("arbitrary", "arbitrary")),
        name="hgproj",
    )(x, m, g, w_in, lb_logits)


def _scan_body(reverse, nsteps, q_ref, k_ref, v_ref, lf_ref, s0_ref, o_ref, sfin_ref, st_ref):
    step = pl.program_id(1)
    tq, d = q_ref.shape[1], q_ref.shape[2]
    nh = d // HEAD
    nchunk = tq // CHUNK

    @pl.when(step == 0)
    def _():
        st_ref[...] = s0_ref[0]

    r = lax.broadcasted_iota(I32, (CHUNK, CHUNK), 0)
    c = lax.broadcasted_iota(I32, (CHUNK, CHUNK), 1)
    keep = (c >= r) if reverse else (c <= r)
    tri = keep.astype(F32)
    tot_row = 0 if reverse else CHUNK - 1
    mid_row = CHUNK // 2
    contract_last = (((1,), (1,)), ((), ()))
    contract_first = (((0,), (0,)), ((), ()))

    order = range(nchunk - 1, -1, -1) if reverse else range(nchunk)
    for ci in order:
        rows = slice(ci * CHUNK, (ci + 1) * CHUNK)
        g = jnp.dot(tri, lf_ref[0, rows, :], precision=HIGHEST, preferred_element_type=F32)
        gtot = g[tot_row:tot_row + 1]
        gmid = g[mid_row:mid_row + 1]
        q = q_ref[0, rows, :].astype(F32)
        k = k_ref[0, rows, :].astype(F32)
        v = v_ref[0, rows, :]
        q_in = (q * jnp.exp(g)).astype(BF16)
        q_mid = (q * jnp.exp(g - gmid)).astype(BF16)
        k_mid = (k * jnp.exp(gmid - g)).astype(BF16)
        k_st = (k * jnp.exp(gtot - g)).astype(BF16)
        d_st = jnp.exp(gtot)
        for h in range(nh):
            sl = slice(h * HEAD, (h + 1) * HEAD)
            st = st_ref[h]
            sc = lax.dot_general(q_mid[:, sl], k_mid[:, sl], contract_last,
                                 preferred_element_type=F32)
            sc = jnp.where(keep, sc, 0.0).astype(BF16)
            o = lax.dot_general(q_in[:, sl], st.astype(BF16), contract_last,
                                preferred_element_type=F32)
            o = o + jnp.dot(sc, v[:, sl], preferred_element_type=F32)
            o_ref[0, rows, sl] = o
            st_ref[h] = st * d_st[:, sl] + lax.dot_general(
                v[:, sl], k_st[:, sl], contract_first, preferred_element_type=F32)

    @pl.when(step == nsteps - 1)
    def _():
        sfin_ref[0] = st_ref[...]


def _scan(q, k, v, lf, s0, reverse, tq):
    bn, t, d = q.shape
    nh = d // HEAD
    nsteps = t // tq
    tmap = (lambda b, i: (b, nsteps - 1 - i, 0)) if reverse else (lambda b, i: (b, i, 0))
    blk = pl.BlockSpec((1, tq, d), tmap)
    sblk = pl.BlockSpec((1, nh, HEAD, HEAD), lambda b, i: (b, 0, 0, 0))
    return pl.pallas_call(
        functools.partial(_scan_body, reverse, nsteps),
        grid=(bn, nsteps),
        in_specs=[blk, blk, blk, blk, sblk],
        out_specs=[blk, sblk],
        out_shape=[jax.ShapeDtypeStruct((bn, t, d), F32),
                   jax.ShapeDtypeStruct((bn, nh, HEAD, HEAD), F32)],
        scratch_shapes=[pltpu.VMEM((nh, HEAD, HEAD), F32)],
        compiler_params=_cparams(("arbitrary", "arbitrary")),
        name="hgscan_bw" if reverse else "hgscan_fw",
    )(q, k, v, lf, s0)


def _hgpost_body(n_exp, of_ref, ob_ref, g_ref, x_ref, m_ref, ng_ref, wo_ref, gffn_ref, rw_ref,
                 lat_o, hx_o, aff_o):
    o = of_ref[0] + ob_ref[0]
    d = o.shape[-1]
    parts = []
    for h in range(d // HEAD):
        oh = o[:, h * HEAD:(h + 1) * HEAD]
        ms = jnp.mean(oh * oh, axis=-1, keepdims=True)
        parts.append(oh * lax.rsqrt(ms + EPS))
    on = jnp.concatenate(parts, axis=1) * ng_ref[...]
    y = (on * _silu(g_ref[0].astype(F32))).astype(BF16)
    z = jnp.dot(y, wo_ref[...], preferred_element_type=F32)
    _post(x_ref[0], z, m_ref[0], gffn_ref[...], rw_ref, n_exp, lat_o, hx_o, aff_o)


def _hgpost(o_fw, o_bw, g, x, m, mrow, ng_tiled, w_out, gffn, rw_pad, n_exp, tt):
    bn, t, d = x.shape
    blk = pl.BlockSpec((1, tt, d), lambda b, i: (b, i, 0))
    row = lambda b, i: (0, 0)
    return pl.pallas_call(
        functools.partial(_hgpost_body, n_exp),
        grid=(bn, t // tt),
        in_specs=[blk, blk, blk, blk,
                  pl.BlockSpec((1, 6, d), lambda b, i: (mrow(b), 0, 0)),
                  pl.BlockSpec((1, d), row), pl.BlockSpec((d, d), row), pl.BlockSpec((1, d), row),
                  pl.BlockSpec((d, LANES), row)],
        out_specs=[blk,
                   pl.BlockSpec((1, tt, d + LANES), lambda b, i: (b, i, 0)),
                   pl.BlockSpec((1, n_exp, tt), lambda b, i: (b, 0, i))],
        out_shape=[jax.ShapeDtypeStruct((bn, t, d), F32),
                   jax.ShapeDtypeStruct((bn, t, d + LANES), F32),
                   jax.ShapeDtypeStruct((bn, n_exp, t), F32)],
        compiler_params=_cparams(("arbitrary", "arbitrary")),
        name="hgpost",
    )(o_fw, o_bw, g, x, m, ng_tiled, w_out, gffn, rw_pad)


def _route_body(n_exp, nb, k_sel, a_ref, idx_o, code_o, lo_o, incl_s, code_s):
    rows = n_exp * nb
    shift = nb.bit_length() - 1
    bits = pltpu.bitcast(a_ref[0], I32)
    b3 = bits.reshape(n_exp, nb, LANES)

    def count(mask3):
        s = jnp.sum(jnp.where(mask3, 1.0, 0.0), axis=1, keepdims=True)
        return jnp.sum(s, axis=2, keepdims=True)

    thr = jnp.zeros((n_exp, 1, 1), I32)
    for bit in range(30, -1, -1):
        cand = thr | (1 << bit)
        thr = jnp.where(count(b3 >= cand) >= k_sel, cand, thr)
    gt3 = b3 > thr
    eq3 = b3 == thr
    need = k_sel - count(gt3)

    ri = lax.broadcasted_iota(I32, (rows, rows), 0)
    ci = lax.broadcasted_iota(I32, (rows, rows), 1)
    lblk = (((ri >> shift) == (ci >> shift)) & (ci < ri)).astype(BF16)
    ui = lax.broadcasted_iota(I32, (LANES, LANES), 0)
    uj = lax.broadcasted_iota(I32, (LANES, LANES), 1)
    upper = (ui <= uj).astype(BF16)
    ones = jnp.ones((LANES, LANES), BF16)

    def cumsum(x):
        xb = x.astype(BF16)
        within = jnp.dot(xb, upper, preferred_element_type=F32)
        part = jnp.dot(lblk, xb, preferred_element_type=F32)
        before = jnp.dot(part.astype(BF16), ones, preferred_element_type=F32)
        return within + before, before

    eqf = jnp.where(eq3, 1.0, 0.0).reshape(rows, LANES)
    eq_incl, _ = cumsum(eqf)
    eq_excl3 = (eq_incl - eqf).reshape(n_exp, nb, LANES)
    sel3 = gt3 | (eq3 & (eq_excl3 < need))
    self = jnp.where(sel3, 1.0, 0.0).reshape(rows, LANES)
    incl, before = cumsum(self)
    incl_s[...] = incl
    lo_o[0] = before.astype(I32)
    code_s[...] = jnp.where(self > 0.0, incl - self, -1.0).astype(I32)

    fill = jnp.full((LANES - n_exp, LANES), -1, I32)
    for blk in range(nb):
        tile = code_s[pl.ds(blk, n_exp, stride=nb), :]
        code_o[0, blk * LANES:(blk + 1) * LANES, :] = jnp.concatenate([tile, fill], axis=0).T

    cap = idx_o.shape[-1]
    rc = min(cap, 256)
    rcol = lax.broadcasted_iota(I32, (rc, LANES), 0).astype(F32)
    ones8 = jnp.ones((8, LANES), BF16)
    contract_last = (((1,), (1,)), ((), ()))

    def per_expert(e, carry):
        for part in range(cap // rc):
            target = rcol + float(part * rc)

            def per_blk(bl, cnt):
                row = incl_s[pl.ds(e * nb + bl, 1), :]
                return cnt + jnp.where(row <= target, 1.0, 0.0)

            cnt = lax.fori_loop(0, nb, per_blk, jnp.zeros((rc, LANES), F32))
            tok = lax.dot_general(ones8, cnt.astype(BF16), contract_last, preferred_element_type=F32)
            idx_o[0, pl.ds(e, 1), part * rc:(part + 1) * rc] = tok[0:1].astype(I32)
        return carry

    lax.fori_loop(0, n_exp, per_expert, 0)


def _route(aff, k_sel):
    bn, n_exp, t = aff.shape
    tp = max(t, 8 * LANES)
    if tp != t:
        aff = jnp.pad(aff, ((0, 0), (0, 0), (0, tp - t)), constant_values=-1.0)
    nb = tp // LANES
    assert nb & (nb - 1) == 0
    rows = n_exp * nb
    a2 = aff.reshape(bn, rows, LANES)
    idx, code, lo = pl.pallas_call(
        functools.partial(_route_body, n_exp, nb, k_sel),
        grid=(bn,),
        in_specs=[pl.BlockSpec((1, rows, LANES), lambda b: (b, 0, 0))],
        out_specs=[pl.BlockSpec((1, n_exp, k_sel), lambda b: (b, 0, 0)),
                   pl.BlockSpec((1, tp, LANES), lambda b: (b, 0, 0)),
                   pl.BlockSpec((1, rows, LANES), lambda b: (b, 0, 0))],
        out_shape=[jax.ShapeDtypeStruct((bn, n_exp, k_sel), I32),
                   jax.ShapeDtypeStruct((bn, tp, LANES), I32),
                   jax.ShapeDtypeStruct((bn, rows, LANES), I32)],
        scratch_shapes=[pltpu.VMEM((rows, LANES), F32), pltpu.VMEM((rows, LANES), I32)],
        compiler_params=_cparams(("arbitrary",)),
        name="route",
    )(a2)
    return idx, code, lo[:, :, 0], nb


def _gather_body(n_exp, ctot, segs, *refs):
    n = len(segs)
    idx_refs, src_refs = refs[:n], refs[n:2 * n]
    dst_ref, sem = refs[2 * n], refs[2 * n + 1]
    step = pl.program_id(0)
    b = step // n_exp
    e = step % n_exp

    def row_copy(s, r):
        cap, t_src, seg = segs[s]
        return pltpu.make_async_copy(
            src_refs[s].at[pl.ds(b * t_src + idx_refs[s][0, 0, r], 1)],
            dst_ref.at[e, pl.ds(b * ctot + seg + r, 1)], sem.at[0])

    for phase in ("start", "wait"):
        for s in range(n):
            def body(r, carry, s=s, phase=phase):
                getattr(row_copy(s, r), phase)()
                return carry

            lax.fori_loop(0, segs[s][0], body, 0, unroll=8)


def _gather(srcs, idxs, ctot):
    bn, n_exp, _ = idxs[0].shape
    dx = srcs[0].shape[-1]
    segs, seg = [], 0
    for src, idx in zip(srcs, idxs):
        segs.append((idx.shape[-1], src.shape[1], seg))
        seg += idx.shape[-1]
    assert seg == ctot
    idx_specs = [pl.BlockSpec((1, 1, idx.shape[-1]), lambda s: (s, 0, 0), memory_space=pltpu.SMEM)
                 for idx in idxs]
    return pl.pallas_call(
        functools.partial(_gather_body, n_exp, ctot, tuple(segs)),
        grid=(bn * n_exp,),
        in_specs=idx_specs + [pl.BlockSpec(memory_space=pl.ANY)] * len(srcs),
        out_specs=pl.BlockSpec(memory_space=pl.ANY),
        out_shape=jax.ShapeDtypeStruct((n_exp, bn * ctot, dx), srcs[0].dtype),
        scratch_shapes=[pltpu.SemaphoreType.DMA((1,))],
        compiler_params=pltpu.CompilerParams(dimension_semantics=("arbitrary",),
                                             has_side_effects=True),
        name="gather",
    )(*[idx.reshape(bn * n_exp, 1, idx.shape[-1]) for idx in idxs],
      *[src.reshape(bn * src.shape[1], dx) for src in srcs])


def _ffn_body(d, nf, x_ref, wg_ref, wu_ref, wd_ref, y_ref, acc_ref):
    e = pl.program_id(0)
    f = pl.program_id(2)
    x = x_ref[0, :, :d].astype(BF16)
    a = jnp.dot(x, wg_ref[0, 0].astype(BF16), preferred_element_type=F32)
    u = jnp.dot(x, wu_ref[0, 0].astype(BF16), preferred_element_type=F32)
    h = (_silu(a) * u).astype(BF16)
    part = jnp.dot(h, wd_ref[0, 0].astype(BF16), preferred_element_type=F32)

    @pl.when(f == 0)
    def _():
        acc_ref[...] = part

    @pl.when(f > 0)
    def _():
        acc_ref[...] += part

    @pl.when(f == nf - 1)
    def _():
        aff = x_ref[0, :, d:]
        lane = lax.broadcasted_iota(I32, aff.shape, 1)
        gate = jnp.sum(jnp.where(lane == e, aff, 0.0), axis=-1, keepdims=True)
        y_ref[0] = (acc_ref[...] * gate).astype(y_ref.dtype)


def _ffn(xg, w_gate, w_up, w_down, layer, tm, fc):
    n_exp, mtot, dx = xg.shape
    d = dx - LANES
    f_dim = w_gate.shape[-1]
    nf = f_dim // fc
    return pl.pallas_call(
        functools.partial(_ffn_body, d, nf),
        grid=(n_exp, mtot // tm, nf),
        in_specs=[pl.BlockSpec((1, tm, dx), lambda e, mi, f: (e, mi, 0)),
                  pl.BlockSpec((1, 1, d, fc), lambda e, mi, f: (layer, e, 0, f)),
                  pl.BlockSpec((1, 1, d, fc), lambda e, mi, f: (layer, e, 0, f)),
                  pl.BlockSpec((1, 1, fc, d), lambda e, mi, f: (layer, e, f, 0))],
        out_specs=pl.BlockSpec((1, tm, d), lambda e, mi, f: (e, mi, 0)),
        out_shape=jax.ShapeDtypeStruct((n_exp, mtot, d), BF16),
        scratch_shapes=[pltpu.VMEM((tm, d), F32)],
        compiler_params=_cparams(("arbitrary", "arbitrary", "arbitrary")),
        name="ffn",
    )(xg, w_gate, w_up, w_down)


def _combine_body(n_exp, nbp, ctot, seg, mtot, final, lo_ref, y_ref, x_ref, code_ref, m_ref,
                  gfin_ref, out_ref, ybuf, acc_ref, sem):
    b = pl.program_id(0)
    i = pl.program_id(1)
    tt, d = acc_ref.shape
    bpt = tt // LANES
    base = b * ctot + seg

    starts = []
    npass = jnp.int32(0)
    for e in range(n_exp):
        o = (b * n_exp + e) * nbp + i * bpt
        r_lo = lo_ref[o]
        r_hi = lo_ref[o + bpt]
        a_e = base + ((r_lo >> 4) << 4)
        starts.append(a_e)
        npass = jnp.maximum(npass, (base + r_hi - a_e + WIN - 1) >> 6)

    acc_ref[...] = jnp.zeros_like(acc_ref)
    code = code_ref[0]
    lane_w = lax.broadcasted_iota(I32, (tt, WIN), 1)

    def one_pass(p, carry):
        want = [a + p * WIN for a in starts]
        rows = [pl.multiple_of(jnp.minimum(w, mtot - WIN), ROW_ALIGN) for w in want]

        def win_copy(e):
            return pltpu.make_async_copy(y_ref.at[e, pl.ds(rows[e], WIN)], ybuf.at[e], sem.at[0])

        for e in range(n_exp):
            win_copy(e).start()
        for e in range(n_exp):
            win_copy(e).wait()
        for g in range(n_exp // 4):
            cols = []
            for j in range(4):
                e = 4 * g + j
                ce = code[:, e:e + 1]
                cabs = ce + base
                hit = (ce >= 0) & (cabs >= want[e]) & ((cabs - rows[e]) == lane_w)
                cols.append(jnp.where(hit, 1.0, 0.0).astype(BF16))
            p_mat = jnp.concatenate(cols, axis=1)
            y_win = ybuf[4 * g:4 * g + 4].reshape(4 * WIN, d)
            acc_ref[...] += jnp.dot(p_mat, y_win, preferred_element_type=F32)
        return carry

    lax.fori_loop(0, npass, one_pass, 0)

    m = m_ref[0]
    lat_new = x_ref[0] + m[5:6] * acc_ref[...]
    if final:
        out_ref[0] = _rms(lat_new, gfin_ref[...])
    else:
        out_ref[0] = lat_new


def _combine(y, x, code, lo_ext, m, mrow, gfin, ctot, seg, final, tt):
    bn, t, d = x.shape
    n_exp, mtot, _ = y.shape
    nbp = lo_ext.shape[-1]
    grid_spec = pltpu.PrefetchScalarGridSpec(
        num_scalar_prefetch=1,
        grid=(bn, t // tt),
        in_specs=[pl.BlockSpec(memory_space=pl.ANY),
                  pl.BlockSpec((1, tt, d), lambda b, i, lo: (b, i, 0)),
                  pl.BlockSpec((1, tt, LANES), lambda b, i, lo: (b, i, 0)),
                  pl.BlockSpec((1, 6, d), lambda b, i, lo: (mrow(b), 0, 0)),
                  pl.BlockSpec((1, d), lambda b, i, lo: (0, 0))],
        out_specs=pl.BlockSpec((1, tt, d), lambda b, i, lo: (b, i, 0)),
        scratch_shapes=[pltpu.VMEM((n_exp, WIN, d), BF16), pltpu.VMEM((tt, d), F32),
                        pltpu.SemaphoreType.DMA((1,))],
    )
    return pl.pallas_call(
        functools.partial(_combine_body, n_exp, nbp, ctot, seg, mtot, final),
        grid_spec=grid_spec,
        out_shape=jax.ShapeDtypeStruct((bn, t, d), F32),
        compiler_params=_cparams(("arbitrary", "arbitrary")),
        name="combine",
    )(lo_ext.reshape(-1), y, x, code, m, gfin)


def _moe(hx_l, aff_l, lat_mid, hx_c, aff_c, ctx_mid, m, layer, wg, wu, wd, final_g, last, tt_l, tt_c):
    bn, t, dx = hx_l.shape
    d = dx - LANES
    n_exp = aff_l.shape[1]
    cap_l = CAPACITY_FACTOR * t // n_exp
    with_ctx = hx_c is not None
    cap_c = CAPACITY_FACTOR * hx_c.shape[1] // n_exp if with_ctx else 0
    ctot = cap_l + cap_c
    mtot = bn * ctot

    def lo_ext(lo, nb, k_sel):
        lo3 = lo.reshape(bn, n_exp, nb)
        return jnp.concatenate([lo3, jnp.full((bn, n_exp, 1), k_sel, I32)], axis=-1)

    idx_l, code_l, lo_l, nb_l = _route(aff_l, cap_l)
    srcs, idxs = [hx_l], [idx_l]
    if with_ctx:
        idx_c, code_c, lo_c, nb_c = _route(aff_c, cap_c)
        srcs.append(hx_c)
        idxs.append(idx_c)
    xg = _gather(srcs, idxs, ctot)
    fc = min(512, wg.shape[-1])
    y = _ffn(xg, wg, wu, wd, layer, ctot, fc)
    lat_new = _combine(y, lat_mid, code_l, lo_ext(lo_l, nb_l, cap_l), m, lambda b: layer * 8 + b,
                       final_g, ctot, 0, last, tt_l)
    ctx_new = None
    if with_ctx:
        ctx_new = _combine(y, ctx_mid, code_c, lo_ext(lo_c, nb_c, cap_c), m,
                           lambda b: layer * 8 + bn, final_g, ctot, cap_l, False, tt_c)
    return lat_new, ctx_new


def kernel(x, c, ctx, c_ctx, mod_w, mod_b, norm_mix_g, norm_ffn_g, cv_w1, cv_b1, cv_dw, cv_dwb,
           cv_ln_g, cv_ln_b, cv_w2, cv_b2, hg_w_in, hg_lb_logits, hg_norm_g, hg_w_out,
           router_w, exp_w_gate, exp_w_up, exp_w_down, final_g):
    bn, t, d = x.shape
    tc = ctx.shape[1]
    depth = mod_w.shape[0]
    n_exp = router_w.shape[-1]
    assert bn < 8 and d % HEAD == 0 and n_exp % 4 == 0 and n_exp <= LANES

    cond8 = jnp.concatenate([c, c_ctx[None], jnp.zeros((8 - bn - 1, d), F32)], axis=0)
    m = _mod(cond8, mod_w, mod_b).reshape(depth * 8, 6, d)

    tt_l = min(512, t)
    tt_c = min(256, tc)
    tcomb_l = min(256, t)
    tcomb_c = min(256, tc)
    row = lambda a: a.reshape(1, -1)
    fin = row(final_g)

    lat, cx = x, ctx
    for i in range(depth):
        last = i == depth - 1
        j = i // 2
        lat_row = lambda b, i=i: i * 8 + b
        ctx_row = lambda b, i=i: i * 8 + bn
        gmix, gffn = row(norm_mix_g[i]), row(norm_ffn_g[i])
        rw_pad = jnp.pad(router_w[i], ((0, 0), (0, LANES - n_exp)))
        need_ctx = not last
        hx_c = aff_c = ctx_mid = None
        if i % 2 == 0:
            w1 = cv_w1[j].astype(BF16)
            w2 = cv_w2[j].astype(BF16)
            cargs = (cv_dw[j], row(cv_dwb[j]), row(cv_ln_g[j]), row(cv_ln_b[j]), w2, row(cv_b2[j]),
                     gffn, rw_pad, n_exp)
            u = _glu(lat, m, lat_row, gmix, w1, row(cv_b1[j]), tt_l)
            lat_mid, hx_l, aff_l = _conv(u, lat, m, lat_row, *cargs, tt_l)
            if need_ctx:
                uc = _glu(cx, m, ctx_row, gmix, w1, row(cv_b1[j]), tt_c)
                ctx_mid, hx_c, aff_c = _conv(uc, cx, m, ctx_row, *cargs, tt_c)
        else:
            w_in = hg_w_in[j].astype(BF16)
            w_out = hg_w_out[j].astype(BF16)
            ng = row(jnp.tile(hg_norm_g[j], d // HEAD))
            s_zero = jnp.zeros((bn, d // HEAD, HEAD, HEAD), F32)
            qc, kfc, kbc, lfc, lbc, vc, gc = _proj(cx, m, ctx_row, gmix, w_in, hg_lb_logits, i, tt_c)
            ql, kfl, kbl, lfl, lbl, vl, gl = _proj(lat, m, lat_row, gmix, w_in, hg_lb_logits, i, tt_l)
            tq_c, tq_l = min(256, tc), min(256, t)
            oc_f, sc_f = _scan(qc, kfc, vc, lfc, s_zero, False, tq_c)
            oc_b, sc_b = _scan(qc, kbc, vc, lbc, s_zero, True, tq_c)
            ol_f, _ = _scan(ql, kfl, vl, lfl, sc_f, False, tq_l)
            ol_b, _ = _scan(ql, kbl, vl, lbl, sc_b, True, tq_l)
            pargs = (ng, w_out, gffn, rw_pad, n_exp)
            lat_mid, hx_l, aff_l = _hgpost(ol_f, ol_b, gl, lat, m, lat_row, *pargs, tt_l)
            if need_ctx:
                ctx_mid, hx_c, aff_c = _hgpost(oc_f, oc_b, gc, cx, m, ctx_row, *pargs, tt_c)
        lat, cx_new = _moe(hx_l, aff_l, lat_mid, hx_c, aff_c, ctx_mid, m, i, exp_w_gate, exp_w_up,
                           exp_w_down, fin, last, tcomb_l, tcomb_c)
        if need_ctx:
            cx = cx_new
    return lat
```

```python
import functools

import jax
import jax.numpy as jnp
from jax import lax
from jax.experimental import pallas as pl
from jax.experimental.pallas import tpu as pltpu

F32 = jnp.float32
BF16 = jnp.bfloat16
I32 = jnp.int32
EPS = 1e-6
HIGHEST = lax.Precision.HIGHEST

LANES = 128
SUBLANES = 8
HEAD = 128
CHUNK = 64
CAPACITY_FACTOR = 2
HALO = 16
WIN = 64
WIN_SHIFT = 6
ROW_ALIGN = 16
ROW_SHIFT = 4
REFINE_STEPS = 26
VMEM_LIMIT_BYTES = 56 * 1024 * 1024
FFN_VMEM_LIMIT_BYTES = 58 * 1024 * 1024

CONTRACT_LAST = (((1,), (1,)), ((), ()))
CONTRACT_FIRST = (((0,), (0,)), ((), ()))


def _cparams(sem):
    return pltpu.CompilerParams(dimension_semantics=sem, vmem_limit_bytes=VMEM_LIMIT_BYTES)


def _silu(x):
    return x * jax.nn.sigmoid(x)


def _rms(x, g):
    ms = jnp.mean(x * x, axis=-1, keepdims=True)
    return x * lax.rsqrt(ms + EPS) * g


def _prenorm(x, g, shift, scale):
    return _rms(x, g) * (1.0 + scale) + shift


def _split_bf16(x):
    hi = x.astype(BF16)
    return hi, (x - hi.astype(F32)).astype(BF16)


def _mod_body(c_ref, w_ref, b_ref, o_ref):
    c = c_ref[...]
    o_ref[0] = jnp.dot(_silu(c), w_ref[0], precision=HIGHEST,
                       preferred_element_type=F32) + b_ref[0]


def _mod(cond8, mod_w, mod_b):
    depth, d, d6 = mod_w.shape
    return pl.pallas_call(
        _mod_body,
        grid=(depth, d6 // d),
        in_specs=[pl.BlockSpec((8, d), lambda l, j: (0, 0)),
                  pl.BlockSpec((1, d, d), lambda l, j: (l, 0, j)),
                  pl.BlockSpec((1, 1, d), lambda l, j: (l, 0, j))],
        out_specs=pl.BlockSpec((1, 8, d), lambda l, j: (l, 0, j)),
        out_shape=jax.ShapeDtypeStruct((depth, 8, d6), F32),
        compiler_params=_cparams(("arbitrary", "arbitrary")),
        name="mod",
    )(cond8, mod_w, mod_b.reshape(depth, 1, d6))


def _post(lat, y, m, gffn, rwh_ref, rwl_ref, n_exp, lat_o, hx_o, aff_o):
    d = lat.shape[-1]
    lat_new = lat + m[2:3] * y
    h = _prenorm(lat_new, gffn, m[3:4], m[4:5])
    h_hi, h_lo = _split_bf16(h)
    logits = (jnp.dot(h_hi, rwh_ref[...], preferred_element_type=F32)
              + jnp.dot(h_lo, rwh_ref[...], preferred_element_type=F32)
              + jnp.dot(h_hi, rwl_ref[...], preferred_element_type=F32))
    lane = lax.broadcasted_iota(I32, logits.shape, 1)
    lg = jnp.where(lane < n_exp, logits, -jnp.inf)
    ex = jnp.exp(lg - jnp.max(lg, axis=-1, keepdims=True))
    aff = ex / jnp.sum(ex, axis=-1, keepdims=True)
    a_hi, a_lo = _split_bf16(aff)
    lat_o[0] = lat_new
    hx_o[0, :, :d] = h_hi
    hx_o[0, :, d:d + LANES] = a_hi
    hx_o[0, :, d + LANES:] = a_lo
    aff_o[0] = aff.T[:n_exp]


def _post_specs(bn, t, d, n_exp, tt):
    specs = [pl.BlockSpec((1, tt, d), lambda b, i: (b, i, 0)),
             pl.BlockSpec((1, tt, d + 2 * LANES), lambda b, i: (b, i, 0)),
             pl.BlockSpec((1, n_exp, tt), lambda b, i: (b, 0, i))]
    shapes = [jax.ShapeDtypeStruct((bn, t, d), F32),
              jax.ShapeDtypeStruct((bn, t, d + 2 * LANES), BF16),
              jax.ShapeDtypeStruct((bn, n_exp, t), F32)]
    return specs, shapes


def _glu_body(x_ref, m_ref, g_ref, w1_ref, b1_ref, u_ref):
    d = x_ref.shape[-1]
    m = m_ref[0]
    h = _prenorm(x_ref[0], g_ref[...], m[0:1], m[1:2]).astype(BF16)
    a = jnp.dot(h, w1_ref[:, :d], preferred_element_type=F32) + b1_ref[:, :d]
    gate = jnp.dot(h, w1_ref[:, d:], preferred_element_type=F32) + b1_ref[:, d:]
    u_ref[0] = (a * jax.nn.sigmoid(gate)).astype(u_ref.dtype)


def _glu(x, m, mrow, g, w1, b1, tt):
    bn, t, d = x.shape
    return pl.pallas_call(
        _glu_body,
        grid=(bn, t // tt),
        in_specs=[pl.BlockSpec((1, tt, d), lambda b, i: (b, i, 0)),
                  pl.BlockSpec((1, 6, d), lambda b, i: (mrow(b), 0, 0)),
                  pl.BlockSpec((1, d), lambda b, i: (0, 0)),
                  pl.BlockSpec((d, 2 * d), lambda b, i: (0, 0)),
                  pl.BlockSpec((1, 2 * d), lambda b, i: (0, 0))],
        out_specs=pl.BlockSpec((1, tt, d), lambda b, i: (b, i, 0)),
        out_shape=jax.ShapeDtypeStruct((bn, t, d), BF16),
        compiler_params=_cparams(("arbitrary", "arbitrary")),
        name="glu",
    )(x, m, g, w1, b1)


def _conv_body(nt, kw, n_exp, up_ref, uc_ref, un_ref, x_ref, m_ref, dw_ref, dwb_ref, lng_ref,
               lnb_ref, w2_ref, b2_ref, gffn_ref, rwh_ref, rwl_ref, lat_o, hx_o, aff_o,
               xs_ref, sh_ref, cv_ref):
    i = pl.program_id(1)
    tt, d = cv_ref.shape
    pad = (kw - 1) // 2
    xs_ref[0:HALO, :] = jnp.where(i > 0, up_ref[0].astype(F32), 0.0)
    xs_ref[HALO:HALO + tt, :] = uc_ref[0].astype(F32)
    xs_ref[HALO + tt:, :] = jnp.where(i < nt - 1, un_ref[0].astype(F32), 0.0)

    strip = min(tt, 128)
    span = sh_ref.shape[1]

    def chan_block(j, carry):
        c0 = pl.multiple_of(j * LANES, LANES)
        for s in range(SUBLANES):
            sh_ref[s] = xs_ref[s:s + span, pl.ds(c0, LANES)]
        for st in range(tt // strip):
            acc = jnp.zeros((strip, LANES), F32)
            for k in range(kw):
                off = k + HALO - pad
                r0 = st * strip + (off // SUBLANES) * SUBLANES
                acc = acc + sh_ref[off % SUBLANES, r0:r0 + strip, :] * dw_ref[k:k + 1, pl.ds(c0, LANES)]
            cv_ref[st * strip:(st + 1) * strip, pl.ds(c0, LANES)] = acc + dwb_ref[:, pl.ds(c0, LANES)]
        return carry

    lax.fori_loop(0, d // LANES, chan_block, 0)

    cv = cv_ref[...]
    mu = jnp.mean(cv, axis=-1, keepdims=True)
    xc = cv - mu
    var = jnp.mean(xc * xc, axis=-1, keepdims=True)
    y = xc * lax.rsqrt(var + EPS) * lng_ref[...] + lnb_ref[...]
    z = jnp.dot(_silu(y).astype(BF16), w2_ref[...], preferred_element_type=F32) + b2_ref[...]
    _post(x_ref[0], z, m_ref[0], gffn_ref[...], rwh_ref, rwl_ref, n_exp, lat_o, hx_o, aff_o)


def _conv(u, x, m, mrow, dw, dwb, lng, lnb, w2, b2, gffn, rwh, rwl, n_exp, tt):
    bn, t, d = x.shape
    nt = t // tt
    kw = dw.shape[0]
    assert (kw - 1) // 2 <= HALO
    hb = tt // HALO
    nhb = t // HALO
    row = lambda b, i: (0, 0)
    out_specs, out_shape = _post_specs(bn, t, d, n_exp, tt)
    return pl.pallas_call(
        functools.partial(_conv_body, nt, kw, n_exp),
        grid=(bn, nt),
        in_specs=[pl.BlockSpec((1, HALO, d), lambda b, i: (b, jnp.maximum(i * hb - 1, 0), 0)),
                  pl.BlockSpec((1, tt, d), lambda b, i: (b, i, 0)),
                  pl.BlockSpec((1, HALO, d), lambda b, i: (b, jnp.minimum((i + 1) * hb, nhb - 1), 0)),
                  pl.BlockSpec((1, tt, d), lambda b, i: (b, i, 0)),
                  pl.BlockSpec((1, 6, d), lambda b, i: (mrow(b), 0, 0)),
                  pl.BlockSpec((kw, d), row),
                  pl.BlockSpec((1, d), row), pl.BlockSpec((1, d), row), pl.BlockSpec((1, d), row),
                  pl.BlockSpec((d, d), row), pl.BlockSpec((1, d), row), pl.BlockSpec((1, d), row),
                  pl.BlockSpec((d, LANES), row), pl.BlockSpec((d, LANES), row)],
        out_specs=out_specs,
        out_shape=out_shape,
        scratch_shapes=[pltpu.VMEM((tt + 2 * HALO, d), F32),
                        pltpu.VMEM((SUBLANES, tt + 2 * HALO - SUBLANES, LANES), F32),
                        pltpu.VMEM((tt, d), F32)],
        compiler_params=_cparams(("arbitrary", "arbitrary")),
        name="conv",
    )(u, u, u, x, m, dw, dwb, lng, lnb, w2, b2, gffn, rwh, rwl)


def _proj_body(layer, x_ref, m_ref, g_ref, w_ref, lbl_ref, q_o, kf_o, kb_o, lf_o, lb_o, v_o, g_o):
    d = x_ref.shape[-1]
    m = m_ref[0]
    h = _prenorm(x_ref[0], g_ref[...], m[0:1], m[1:2]).astype(BF16)
    lg = lbl_ref[...]
    ex = jnp.exp(lg - jnp.max(lg, axis=0, keepdims=True))
    soft = ex / jnp.sum(ex, axis=0, keepdims=True)
    lbd = jnp.zeros(soft.shape[1:], F32)
    for l in range(1, layer + 1):
        lbd = lbd + soft[l]

    def col(j):
        return jnp.dot(h, w_ref[:, j * d:(j + 1) * d], preferred_element_type=F32)

    q_o[0] = col(0).astype(q_o.dtype)
    for dr, (k_o, l_o) in enumerate(((kf_o, lf_o), (kb_o, lb_o))):
        lb_row = lbd[dr:dr + 1]
        fg = lb_row + (1.0 - lb_row) * jax.nn.sigmoid(col(1 + dr))
        k_o[0] = (1.0 - fg).astype(k_o.dtype)
        l_o[0] = jnp.log(fg)
    v_o[0] = col(3).astype(v_o.dtype)
    g_o[0] = col(4).astype(g_o.dtype)


def _proj(x, m, mrow, g, w_in, lb_logits, layer, tt):
    bn, t, d = x.shape
    blk = pl.BlockSpec((1, tt, d), lambda b, i: (b, i, 0))
    sd = lambda dt: jax.ShapeDtypeStruct((bn, t, d), dt)
    return pl.pallas_call(
        functools.partial(_proj_body, layer),
        grid=(bn, t // tt),
        in_specs=[blk,
                  pl.BlockSpec((1, 6, d), lambda b, i: (mrow(b), 0, 0)),
                  pl.BlockSpec((1, d), lambda b, i: (0, 0)),
                  pl.BlockSpec((d, 5 * d), lambda b, i: (0, 0)),
                  pl.BlockSpec(lb_logits.shape, lambda b, i: (0, 0, 0))],
        out_specs=[blk] * 7,
        out_shape=[sd(BF16), sd(BF16), sd(BF16), sd(F32), sd(F32), sd(BF16), sd(BF16)],
        compiler_params=_cparams(("arbitrary", "arbitrary")),
        name="hgproj",
    )(x, m, g, w_in, lb_logits)


def _scan_chunk(reverse, q_ref, k_ref, v_ref, lf_ref, o_ref, st_ref, rows):
    d = q_ref.shape[2]
    r = lax.broadcasted_iota(I32, (CHUNK, CHUNK), 0)
    c = lax.broadcasted_iota(I32, (CHUNK, CHUNK), 1)
    keep = (c >= r) if reverse else (c <= r)
    tri = keep.astype(BF16)
    tot_row = 0 if reverse else CHUNK - 1
    mid_row = CHUNK // 2

    lf = lf_ref[0, rows, :]
    lf_hi = lf.astype(BF16)
    lf_mid, lf_lo = _split_bf16(lf - lf_hi.astype(F32))
    g = (jnp.dot(tri, lf_hi, preferred_element_type=F32)
         + jnp.dot(tri, lf_mid, preferred_element_type=F32)
         + jnp.dot(tri, lf_lo, preferred_element_type=F32))
    gtot = g[tot_row:tot_row + 1]
    gmid = g[mid_row:mid_row + 1]
    q = q_ref[0, rows, :].astype(F32)
    k = k_ref[0, rows, :].astype(F32)
    v = v_ref[0, rows, :]
    q_in = (q * jnp.exp(g)).astype(BF16)
    q_mid = (q * jnp.exp(g - gmid)).astype(BF16)
    k_mid = (k * jnp.exp(gmid - g)).astype(BF16)
    k_st = (k * jnp.exp(gtot - g)).astype(BF16)
    d_st = jnp.exp(gtot)
    for h in range(d // HEAD):
        sl = slice(h * HEAD, (h + 1) * HEAD)
        st = st_ref[h]
        sc = lax.dot_general(q_mid[:, sl], k_mid[:, sl], CONTRACT_LAST, preferred_element_type=F32)
        sc = jnp.where(keep, sc, 0.0).astype(BF16)
        o = lax.dot_general(q_in[:, sl], st.astype(BF16), CONTRACT_LAST, preferred_element_type=F32)
        o = o + jnp.dot(sc, v[:, sl], preferred_element_type=F32)
        o_ref[0, rows, sl] = o
        st_ref[h] = st * d_st[:, sl] + lax.dot_general(
            v[:, sl], k_st[:, sl], CONTRACT_FIRST, preferred_element_type=F32)


def _scan_body(nsteps, qf_ref, kf_ref, vf_ref, lf_ref, qb_ref, kb_ref, vb_ref, lb_ref, s0f_ref,
               s0b_ref, of_ref, ob_ref, sff_ref, sfb_ref, stf_ref, stb_ref):
    step = pl.program_id(1)
    nchunk = qf_ref.shape[1] // CHUNK

    @pl.when(step == 0)
    def _():
        stf_ref[...] = s0f_ref[0]
        stb_ref[...] = s0b_ref[0]

    for ci in range(nchunk):
        cf, cb = ci, nchunk - 1 - ci
        _scan_chunk(False, qf_ref, kf_ref, vf_ref, lf_ref, of_ref, stf_ref,
                    slice(cf * CHUNK, (cf + 1) * CHUNK))
        _scan_chunk(True, qb_ref, kb_ref, vb_ref, lb_ref, ob_ref, stb_ref,
                    slice(cb * CHUNK, (cb + 1) * CHUNK))

    @pl.when(step == nsteps - 1)
    def _():
        sff_ref[0] = stf_ref[...]
        sfb_ref[0] = stb_ref[...]


def _scan(q, kf, kb, v, lf, lb, s0f, s0b, tq):
    bn, t, d = q.shape
    nh = d // HEAD
    nsteps = t // tq
    fblk = pl.BlockSpec((1, tq, d), lambda b, i: (b, i, 0))
    bblk = pl.BlockSpec((1, tq, d), lambda b, i: (b, nsteps - 1 - i, 0))
    sblk = pl.BlockSpec((1, nh, HEAD, HEAD), lambda b, i: (b, 0, 0, 0))
    st_shape = jax.ShapeDtypeStruct((bn, nh, HEAD, HEAD), F32)
    o_shape = jax.ShapeDtypeStruct((bn, t, d), F32)
    return pl.pallas_call(
        functools.partial(_scan_body, nsteps),
        grid=(bn, nsteps),
        in_specs=[fblk] * 4 + [bblk] * 4 + [sblk, sblk],
        out_specs=[fblk, bblk, sblk, sblk],
        out_shape=[o_shape, o_shape, st_shape, st_shape],
        scratch_shapes=[pltpu.VMEM((nh, HEAD, HEAD), F32), pltpu.VMEM((nh, HEAD, HEAD), F32)],
        compiler_params=_cparams(("arbitrary", "arbitrary")),
        name="hgscan",
    )(q, kf, v, lf, q, kb, v, lb, s0f, s0b)


def _hgpost_body(n_exp, of_ref, ob_ref, g_ref, x_ref, m_ref, ng_ref, wo_ref, gffn_ref, rwh_ref,
                 rwl_ref, lat_o, hx_o, aff_o):
    o = of_ref[0] + ob_ref[0]
    d = o.shape[-1]
    parts = []
    for h in range(d // HEAD):
        oh = o[:, h * HEAD:(h + 1) * HEAD]
        ms = jnp.mean(oh * oh, axis=-1, keepdims=True)
        parts.append(oh * lax.rsqrt(ms + EPS))
    on = jnp.concatenate(parts, axis=1) * ng_ref[...]
    y = (on * _silu(g_ref[0].astype(F32))).astype(BF16)
    z = jnp.dot(y, wo_ref[...], preferred_element_type=F32)
    _post(x_ref[0], z, m_ref[0], gffn_ref[...], rwh_ref, rwl_ref, n_exp, lat_o, hx_o, aff_o)


def _hgpost(o_fw, o_bw, g, x, m, mrow, ng_tiled, w_out, gffn, rwh, rwl, n_exp, tt):
    bn, t, d = x.shape
    blk = pl.BlockSpec((1, tt, d), lambda b, i: (b, i, 0))
    row = lambda b, i: (0, 0)
    out_specs, out_shape = _post_specs(bn, t, d, n_exp, tt)
    return pl.pallas_call(
        functools.partial(_hgpost_body, n_exp),
        grid=(bn, t // tt),
        in_specs=[blk, blk, blk, blk,
                  pl.BlockSpec((1, 6, d), lambda b, i: (mrow(b), 0, 0)),
                  pl.BlockSpec((1, d), row), pl.BlockSpec((d, d), row), pl.BlockSpec((1, d), row),
                  pl.BlockSpec((d, LANES), row), pl.BlockSpec((d, LANES), row)],
        out_specs=out_specs,
        out_shape=out_shape,
        compiler_params=_cparams(("arbitrary", "arbitrary")),
        name="hgpost",
    )(o_fw, o_bw, g, x, m, ng_tiled, w_out, gffn, rwh, rwl)


def _route_body(n_exp, nb, k_sel, a_ref, code_e_o, code_t_o, lo_o, code_s):
    rows = n_exp * nb
    shift = nb.bit_length() - 1
    a3 = a_ref[0].reshape(n_exp, nb, LANES)

    def count(mask3):
        s = jnp.sum(jnp.where(mask3, 1.0, 0.0), axis=1, keepdims=True)
        return jnp.sum(s, axis=2, keepdims=True)

    thr = jnp.zeros((n_exp, 1, 1), I32)
    for bit in range(30, -1, -1):
        cand = thr | (1 << bit)
        enough = count(a3 >= lax.bitcast_convert_type(cand, F32)) >= k_sel
        thr = jnp.where(enough, cand, thr)
    lo = lax.bitcast_convert_type(thr, F32)
    hi = lax.bitcast_convert_type(thr + 1, F32)
    for _ in range(REFINE_STEPS):
        mid = 0.5 * (lo + hi)
        enough = count(a3 >= mid) >= k_sel
        lo = jnp.where(enough, mid, lo)
        hi = jnp.where(enough, hi, mid)
    gt3 = a3 >= hi
    eq3 = (a3 >= lo) & jnp.logical_not(gt3)
    need = k_sel - count(gt3)

    ri = lax.broadcasted_iota(I32, (rows, rows), 0)
    ci = lax.broadcasted_iota(I32, (rows, rows), 1)
    lblk = (((ri >> shift) == (ci >> shift)) & (ci < ri)).astype(BF16)
    ui = lax.broadcasted_iota(I32, (LANES, LANES), 0)
    uj = lax.broadcasted_iota(I32, (LANES, LANES), 1)
    upper = (ui <= uj).astype(BF16)
    ones = jnp.ones((LANES, LANES), BF16)

    def cumsum(x):
        xb = x.astype(BF16)
        within = jnp.dot(xb, upper, preferred_element_type=F32)
        part = jnp.dot(lblk, xb, preferred_element_type=F32)
        before = jnp.dot(part.astype(BF16), ones, preferred_element_type=F32)
        return within + before, before

    eqf = jnp.where(eq3, 1.0, 0.0).reshape(rows, LANES)
    eq_incl, _ = cumsum(eqf)
    eq_excl3 = (eq_incl - eqf).reshape(n_exp, nb, LANES)
    sel3 = gt3 | (eq3 & (eq_excl3 < need))
    self = jnp.where(sel3, 1.0, 0.0).reshape(rows, LANES)
    incl, before = cumsum(self)
    lo_o[0] = before.astype(I32)
    code = jnp.where(self > 0.0, incl - self, -1.0).astype(I32)
    code_e_o[0] = code
    code_s[...] = code

    fill = jnp.full((LANES - n_exp, LANES), -1, I32)
    for blk in range(nb):
        tile = code_s[pl.ds(blk, n_exp, stride=nb), :]
        code_t_o[0, blk * LANES:(blk + 1) * LANES, :] = jnp.concatenate([tile, fill], axis=0).T


def _route(aff, k_sel):
    bn, n_exp, t = aff.shape
    tp = max(t, SUBLANES * LANES)
    if tp != t:
        aff = jnp.pad(aff, ((0, 0), (0, 0), (0, tp - t)), constant_values=-1.0)
    nb = tp // LANES
    assert nb & (nb - 1) == 0
    rows = n_exp * nb
    blk = pl.BlockSpec((1, rows, LANES), lambda b: (b, 0, 0))
    code_e, code_t, lo = pl.pallas_call(
        functools.partial(_route_body, n_exp, nb, k_sel),
        grid=(bn,),
        in_specs=[blk],
        out_specs=[blk, pl.BlockSpec((1, tp, LANES), lambda b: (b, 0, 0)), blk],
        out_shape=[jax.ShapeDtypeStruct((bn, rows, LANES), I32),
                   jax.ShapeDtypeStruct((bn, tp, LANES), I32),
                   jax.ShapeDtypeStruct((bn, rows, LANES), I32)],
        scratch_shapes=[pltpu.VMEM((rows, LANES), I32)],
        compiler_params=_cparams(("arbitrary",)),
        name="route",
    )(aff.reshape(bn, rows, LANES))
    lo3 = lo[:, :, 0].reshape(bn, n_exp, nb)
    lo_ext = jnp.concatenate([lo3, jnp.full((bn, n_exp, 1), k_sel, I32)], axis=-1)
    return code_e.reshape(bn, n_exp, tp), code_t, lo_ext.reshape(-1), nb + 1


def _dispatch_body(n_exp, nbp, cap, lo_ref, hx_ref, code_ref, x_ref, xwin, carry, pend, sem):
    b = pl.program_id(0)
    i = pl.program_id(1)
    tt = hx_ref.shape[1]
    bpt = tt // LANES
    base = b * cap
    is_last = (b == pl.num_programs(0) - 1) & (i == pl.num_programs(1) - 1)

    @pl.when(i == 0)
    def _():
        carry[...] = jnp.zeros_like(carry)

    def out_copy(e, row):
        return pltpu.make_async_copy(xwin.at[e], x_ref.at[e, pl.ds(row, WIN)], sem.at[0])

    def drain():
        for e in range(n_exp):
            @pl.when(pend[e] == 1)
            def _(e=e):
                out_copy(e, 0).wait()
                pend[e] = 0

    @pl.when((b == 0) & (i == 0))
    def _():
        xwin[...] = jnp.zeros_like(xwin)
        for e in range(n_exp):
            out_copy(e, pl.num_programs(0) * cap).start()
            pend[e] = 1

    starts, next_blk = [], []
    npass = jnp.int32(0)
    for e in range(n_exp):
        o = (b * n_exp + e) * nbp + i * bpt
        a_e = (lo_ref[o] >> ROW_SHIFT) << ROW_SHIFT
        c_e = ((lo_ref[o + bpt] >> ROW_SHIFT) << ROW_SHIFT) - a_e
        starts.append(a_e)
        next_blk.append(c_e)
        npass = jnp.maximum(npass, (c_e >> WIN_SHIFT) + 1)

    h = hx_ref[0]
    code = code_ref[0]
    jrow = lax.broadcasted_iota(I32, (WIN, tt), 0)

    def one_pass(p, loop_carry):
        onehot = []
        for e in range(n_exp):
            rel = code[e:e + 1, :] - (starts[e] + p * WIN)
            onehot.append(jnp.where(rel == jrow, 1.0, 0.0).astype(BF16))
        xw = jnp.dot(jnp.concatenate(onehot, axis=0), h, preferred_element_type=F32)
        drain()
        for e in range(n_exp):
            xwin[e] = xw[e * WIN:(e + 1) * WIN].astype(BF16)

        @pl.when(p == 0)
        def _():
            for e in range(n_exp):
                xwin[e, 0:ROW_ALIGN, :] = xwin[e, 0:ROW_ALIGN, :] + carry[e]

        for e in range(n_exp):
            off = next_blk[e] - p * WIN

            @pl.when((off >= 0) & (off < WIN))
            def _(e=e, off=off):
                carry[e] = xwin[e, pl.ds(pl.multiple_of(off, ROW_ALIGN), ROW_ALIGN), :]

            @pl.when(off >= 0)
            def _(e=e):
                out_copy(e, pl.multiple_of(base + starts[e] + p * WIN, ROW_ALIGN)).start()
                pend[e] = 1
        return loop_carry

    lax.fori_loop(0, npass, one_pass, 0)

    @pl.when(is_last)
    def _():
        drain()


def _dispatch(hx, code_e, lo_ext, nbp, cap, tt):
    bn, t, dx = hx.shape
    n_exp = code_e.shape[1]
    assert cap % ROW_ALIGN == 0
    grid_spec = pltpu.PrefetchScalarGridSpec(
        num_scalar_prefetch=1,
        grid=(bn, t // tt),
        in_specs=[pl.BlockSpec((1, tt, dx), lambda b, i, lo: (b, i, 0)),
                  pl.BlockSpec((1, n_exp, tt), lambda b, i, lo: (b, 0, i))],
        out_specs=pl.BlockSpec(memory_space=pl.ANY),
        scratch_shapes=[pltpu.VMEM((n_exp, WIN, dx), BF16), pltpu.VMEM((n_exp, ROW_ALIGN, dx), BF16),
                        pltpu.SMEM((n_exp,), I32), pltpu.SemaphoreType.DMA((1,))],
    )
    return pl.pallas_call(
        functools.partial(_dispatch_body, n_exp, nbp, cap),
        grid_spec=grid_spec,
        out_shape=jax.ShapeDtypeStruct((n_exp, bn * cap + WIN, dx), BF16),
        compiler_params=pltpu.CompilerParams(dimension_semantics=("arbitrary", "arbitrary"),
                                             vmem_limit_bytes=VMEM_LIMIT_BYTES,
                                             has_side_effects=True),
        name="dispatch",
    )(lo_ext, hx, code_e)


def _ffn_body(d, nf, with_ctx, *refs):
    if with_ctx:
        (x_ref, xc_ref, wg_ref, wu_ref, wd_ref, y_ref, yc_ref,
         wg_s, wu_s, wd_s, acc_ref, accc_ref) = refs
    else:
        x_ref, wg_ref, wu_ref, wd_ref, y_ref, wg_s, wu_s, wd_s, acc_ref = refs
    e = pl.program_id(0)
    f = pl.program_id(1)
    mi = pl.program_id(2)

    @pl.when(mi == 0)
    def _():
        wg_s[...] = wg_ref[0, 0].astype(BF16)
        wu_s[...] = wu_ref[0, 0].astype(BF16)
        wd_s[...] = wd_ref[0, 0].astype(BF16)

    @pl.when((mi == 0) & (f == 0))
    def _():
        y_ref[...] = jnp.zeros_like(y_ref)

    def run(xr, yr, acc):
        x = xr[0, :, :d]
        a = jnp.dot(x, wg_s[...], preferred_element_type=F32)
        u = jnp.dot(x, wu_s[...], preferred_element_type=F32)
        part = jnp.dot((_silu(a) * u).astype(BF16), wd_s[...], preferred_element_type=F32)

        @pl.when(f == 0)
        def _():
            acc[...] = part

        @pl.when(f > 0)
        def _():
            acc[...] += part

        @pl.when(f == nf - 1)
        def _():
            aff = xr[0, :, d:d + LANES].astype(F32) + xr[0, :, d + LANES:].astype(F32)
            lane = lax.broadcasted_iota(I32, aff.shape, 1)
            gate = jnp.sum(jnp.where(lane == e, aff, 0.0), axis=-1, keepdims=True)
            yr[0] = (acc[...] * gate).astype(yr.dtype)

    run(x_ref, y_ref, acc_ref.at[mi])
    if with_ctx:
        @pl.when(mi == 0)
        def _():
            run(xc_ref, yc_ref, accc_ref)


def _ffn(xl, xc, w_gate, w_up, w_down, layer, tm, tmc, fc):
    n_exp, _, dx = xl.shape
    d = dx - 2 * LANES
    nm = (xl.shape[1] - WIN) // tm
    nf = w_gate.shape[-1] // fc
    with_ctx = xc is not None
    in_specs = [pl.BlockSpec((1, tm, dx), lambda e, f, mi: (e, mi, 0))]
    out_specs = [pl.BlockSpec((1, tm, d), lambda e, f, mi: (e, jnp.where(f == nf - 1, mi, nm), 0))]
    out_shape = [jax.ShapeDtypeStruct((n_exp, (nm + 1) * tm, d), BF16)]
    scratch = [pltpu.VMEM((d, fc), BF16), pltpu.VMEM((d, fc), BF16), pltpu.VMEM((fc, d), BF16),
               pltpu.VMEM((nm, tm, d), F32)]
    args = [xl]
    if with_ctx:
        in_specs.append(pl.BlockSpec((1, tmc, dx), lambda e, f, mi: (e, 0, 0)))
        out_specs.append(pl.BlockSpec((1, tmc, d), lambda e, f, mi: (e, 0, 0)))
        out_shape.append(jax.ShapeDtypeStruct((n_exp, tmc, d), BF16))
        scratch.append(pltpu.VMEM((tmc, d), F32))
        args.append(xc)
    in_specs += [pl.BlockSpec((1, 1, d, fc), lambda e, f, mi: (layer, e, 0, f)),
                 pl.BlockSpec((1, 1, d, fc), lambda e, f, mi: (layer, e, 0, f)),
                 pl.BlockSpec((1, 1, fc, d), lambda e, f, mi: (layer, e, f, 0))]
    outs = pl.pallas_call(
        functools.partial(_ffn_body, d, nf, with_ctx),
        grid=(n_exp, nf, nm),
        in_specs=in_specs,
        out_specs=out_specs,
        out_shape=out_shape,
        scratch_shapes=scratch,
        compiler_params=pltpu.CompilerParams(
            dimension_semantics=("arbitrary", "arbitrary", "arbitrary"),
            vmem_limit_bytes=FFN_VMEM_LIMIT_BYTES),
        name="ffn",
    )(*args, w_gate, w_up, w_down)
    return (outs[0], outs[1], nm * tm) if with_ctx else (outs[0], None, nm * tm)


def _combine_body(n_exp, nbp, cap, mtot, final, lo_ref, y_ref, x_ref, code_ref, m_ref,
                  gfin_ref, out_ref, ybuf, yextra, acc_ref, sem):
    b = pl.program_id(0)
    i = pl.program_id(1)
    nt = pl.num_programs(1)
    step = b * nt + i
    slot = step % 2
    tt, d = acc_ref.shape
    bpt = tt // LANES
    base = b * cap

    def tile_windows(bb, ii):
        firsts, npass = [], jnp.int32(0)
        for e in range(n_exp):
            o = (bb * n_exp + e) * nbp + ii * bpt
            a_e = bb * cap + ((lo_ref[o] >> ROW_SHIFT) << ROW_SHIFT)
            firsts.append(a_e)
            npass = jnp.maximum(npass, (bb * cap + lo_ref[o + bpt] - a_e + WIN - 1) >> WIN_SHIFT)
        return firsts, npass

    def clamp(w):
        return pl.multiple_of(jnp.minimum(w, mtot - WIN), ROW_ALIGN)

    def win_copy(e, row, buf, k):
        return pltpu.make_async_copy(y_ref.at[e, pl.ds(row, WIN)], buf.at[e], sem.at[k])

    starts, npass = tile_windows(b, i)

    @pl.when(step == 0)
    def _():
        for e in range(n_exp):
            win_copy(e, clamp(starts[e]), ybuf.at[0], 0).start()

    @pl.when(step + 1 < pl.num_programs(0) * nt)
    def _():
        wrap = i + 1 == nt
        nxt, _ = tile_windows(jnp.where(wrap, b + 1, b), jnp.where(wrap, 0, i + 1))
        for e in range(n_exp):
            win_copy(e, clamp(nxt[e]), ybuf.at[1 - slot], 1 - slot).start()

    acc_ref[...] = jnp.zeros_like(acc_ref)
    code = code_ref[0]
    lane_w = lax.broadcasted_iota(I32, (tt, WIN), 1)

    def expand(buf, want, rows):
        for g in range(n_exp // 4):
            cols = []
            for j in range(4):
                e = 4 * g + j
                ce = code[:, e:e + 1]
                cabs = ce + base
                hit = (ce >= 0) & (cabs >= want[e]) & ((cabs - rows[e]) == lane_w)
                cols.append(jnp.where(hit, 1.0, 0.0).astype(BF16))
            p_mat = jnp.concatenate(cols, axis=1)
            y_win = buf[4 * g:4 * g + 4].reshape(4 * WIN, d)
            acc_ref[...] += jnp.dot(p_mat, y_win, preferred_element_type=F32)

    rows0 = [clamp(a) for a in starts]
    for e in range(n_exp):
        win_copy(e, rows0[e], ybuf.at[slot], slot).wait()
    expand(ybuf.at[slot], starts, rows0)

    def later_pass(p, carry):
        want = [a + p * WIN for a in starts]
        rows = [clamp(w) for w in want]
        for e in range(n_exp):
            win_copy(e, rows[e], yextra, 2).start()
        for e in range(n_exp):
            win_copy(e, rows[e], yextra, 2).wait()
        expand(yextra, want, rows)
        return carry

    lax.fori_loop(1, npass, later_pass, 0)

    m = m_ref[0]
    lat_new = x_ref[0] + m[5:6] * acc_ref[...]
    if final:
        out_ref[0] = _rms(lat_new, gfin_ref[...])
    else:
        out_ref[0] = lat_new


def _combine(y, mtot, x, code_t, lo_ext, nbp, m, mrow, gfin, cap, final, tt):
    bn, t, d = x.shape
    n_exp = y.shape[0]
    grid_spec = pltpu.PrefetchScalarGridSpec(
        num_scalar_prefetch=1,
        grid=(bn, t // tt),
        in_specs=[pl.BlockSpec(memory_space=pl.ANY),
                  pl.BlockSpec((1, tt, d), lambda b, i, lo: (b, i, 0)),
                  pl.BlockSpec((1, tt, LANES), lambda b, i, lo: (b, i, 0)),
                  pl.BlockSpec((1, 6, d), lambda b, i, lo: (mrow(b), 0, 0)),
                  pl.BlockSpec((1, d), lambda b, i, lo: (0, 0))],
        out_specs=pl.BlockSpec((1, tt, d), lambda b, i, lo: (b, i, 0)),
        scratch_shapes=[pltpu.VMEM((2, n_exp, WIN, d), BF16), pltpu.VMEM((n_exp, WIN, d), BF16),
                        pltpu.VMEM((tt, d), F32), pltpu.SemaphoreType.DMA((3,))],
    )
    return pl.pallas_call(
        functools.partial(_combine_body, n_exp, nbp, cap, mtot, final),
        grid_spec=grid_spec,
        out_shape=jax.ShapeDtypeStruct((bn, t, d), F32),
        compiler_params=_cparams(("arbitrary", "arbitrary")),
        name="combine",
    )(lo_ext, y, x, code_t, m, gfin)


def _moe(hx_l, aff_l, lat_mid, hx_c, aff_c, ctx_mid, m, layer, wg, wu, wd, final_g, last, tt_l, tt_c):
    bn, t, _ = hx_l.shape
    n_exp = aff_l.shape[1]
    with_ctx = hx_c is not None
    cap_l = CAPACITY_FACTOR * t // n_exp
    code_e_l, code_t_l, lo_l, nbp_l = _route(aff_l, cap_l)
    xl = _dispatch(hx_l, code_e_l, lo_l, nbp_l, cap_l, tt_l)
    xc, cap_c = None, 0
    if with_ctx:
        cap_c = CAPACITY_FACTOR * hx_c.shape[1] // n_exp
        code_e_c, code_t_c, lo_c, nbp_c = _route(aff_c, cap_c)
        xc = _dispatch(hx_c, code_e_c, lo_c, nbp_c, cap_c, tt_c)
    fc = min(512, wg.shape[-1])
    yl, yc, rows_l = _ffn(xl, xc, wg, wu, wd, layer, cap_l, bn * cap_c, fc)
    lat_new = _combine(yl, rows_l, lat_mid, code_t_l, lo_l, nbp_l, m, lambda b: layer * 8 + b,
                       final_g, cap_l, last, tt_l)
    ctx_new = None
    if with_ctx:
        ctx_new = _combine(yc, bn * cap_c, ctx_mid, code_t_c, lo_c, nbp_c, m,
                           lambda b: layer * 8 + bn, final_g, cap_c, False, tt_c)
    return lat_new, ctx_new


def kernel(x, c, ctx, c_ctx, mod_w, mod_b, norm_mix_g, norm_ffn_g, cv_w1, cv_b1, cv_dw, cv_dwb,
           cv_ln_g, cv_ln_b, cv_w2, cv_b2, hg_w_in, hg_lb_logits, hg_norm_g, hg_w_out,
           router_w, exp_w_gate, exp_w_up, exp_w_down, final_g):
    bn, t, d = x.shape
    tc = ctx.shape[1]
    depth = mod_w.shape[0]
    n_exp = router_w.shape[-1]
    assert bn < 8 and d % HEAD == 0 and n_exp % 4 == 0 and n_exp <= LANES

    cond8 = jnp.concatenate([c, c_ctx[None], jnp.zeros((8 - bn - 1, d), F32)], axis=0)
    m = _mod(cond8, mod_w, mod_b).reshape(depth * 8, 6, d)

    tt_l = min(512, t)
    tt_c = min(256, tc)
    tmoe_l = min(256, t)
    tmoe_c = min(256, tc)
    row = lambda a: a.reshape(1, -1)
    fin = row(final_g)

    lat, cx = x, ctx
    for i in range(depth):
        last = i == depth - 1
        j = i // 2
        lat_row = lambda b, i=i: i * 8 + b
        ctx_row = lambda b, i=i: i * 8 + bn
        gmix, gffn = row(norm_mix_g[i]), row(norm_ffn_g[i])
        rw_pad = jnp.pad(router_w[i], ((0, 0), (0, LANES - n_exp)))
        rwh = rw_pad.astype(BF16)
        rwl = (rw_pad - rwh.astype(F32)).astype(BF16)
        need_ctx = not last
        hx_c = aff_c = ctx_mid = None
        if i % 2 == 0:
            w1 = cv_w1[j].astype(BF16)
            w2 = cv_w2[j].astype(BF16)
            cargs = (cv_dw[j], row(cv_dwb[j]), row(cv_ln_g[j]), row(cv_ln_b[j]), w2, row(cv_b2[j]),
                     gffn, rwh, rwl, n_exp)
            u = _glu(lat, m, lat_row, gmix, w1, row(cv_b1[j]), tt_l)
            lat_mid, hx_l, aff_l = _conv(u, lat, m, lat_row, *cargs, tt_l)
            if need_ctx:
                uc = _glu(cx, m, ctx_row, gmix, w1, row(cv_b1[j]), tt_c)
                ctx_mid, hx_c, aff_c = _conv(uc, cx, m, ctx_row, *cargs, tt_c)
        else:
            w_in = hg_w_in[j].astype(BF16)
            w_out = hg_w_out[j].astype(BF16)
            ng = row(jnp.tile(hg_norm_g[j], d // HEAD))
            s_zero = jnp.zeros((bn, d // HEAD, HEAD, HEAD), F32)
            qc, kfc, kbc, lfc, lbc, vc, gc = _proj(cx, m, ctx_row, gmix, w_in, hg_lb_logits, i, tt_c)
            ql, kfl, kbl, lfl, lbl, vl, gl = _proj(lat, m, lat_row, gmix, w_in, hg_lb_logits, i, tt_l)
            tq_c, tq_l = min(256, tc), min(256, t)
            oc_f, oc_b, sc_f, sc_b = _scan(qc, kfc, kbc, vc, lfc, lbc, s_zero, s_zero, tq_c)
            ol_f, ol_b, _, _ = _scan(ql, kfl, kbl, vl, lfl, lbl, sc_f, sc_b, tq_l)
            pargs = (ng, w_out, gffn, rwh, rwl, n_exp)
            lat_mid, hx_l, aff_l = _hgpost(ol_f, ol_b, gl, lat, m, lat_row, *pargs, tt_l)
            if need_ctx:
                ctx_mid, hx_c, aff_c = _hgpost(oc_f, oc_b, gc, cx, m, ctx_row, *pargs, tt_c)
        lat, cx_new = _moe(hx_l, aff_l, lat_mid, hx_c, aff_c, ctx_mid, m, i, exp_w_gate, exp_w_up,
                           exp_w_down, fin, last, tmoe_l, tmoe_c)
        if need_ctx:
            cx = cx_new
    return lat
```

```python
import functools

import jax
import jax.numpy as jnp
from jax import lax
from jax.experimental import pallas as pl
from jax.experimental.pallas import tpu as pltpu

F32 = jnp.float32
BF16 = jnp.bfloat16
I32 = jnp.int32
EPS = 1e-6
HIGHEST = lax.Precision.HIGHEST

LANES = 128
SUBLANES = 8
HEAD = 128
CHUNK = 128
HALF = CHUNK // 2
EXP_RANGE = 80.0
CAPACITY_FACTOR = 2
HALO = 16
WIN = 64
WIN_SHIFT = 6
ROW_ALIGN = 16
ROW_SHIFT = 4
REFINE_STEPS = 26
CODE_SPLIT_SHIFT = 5
VMEM_LIMIT_BYTES = 56 * 1024 * 1024

CONTRACT_LAST = (((1,), (1,)), ((), ()))
CONTRACT_FIRST = (((0,), (0,)), ((), ()))


def _cparams(sem):
    return pltpu.CompilerParams(dimension_semantics=sem, vmem_limit_bytes=VMEM_LIMIT_BYTES)


def _silu(x):
    return x * jax.nn.sigmoid(x)


def _rms(x, g):
    ms = jnp.mean(x * x, axis=-1, keepdims=True)
    return x * lax.rsqrt(ms + EPS) * g


def _prenorm(x, g, shift, scale):
    return _rms(x, g) * (1.0 + scale) + shift


def _split_bf16(x):
    hi = x.astype(BF16)
    return hi, (x - hi.astype(F32)).astype(BF16)


def _mod_body(c_ref, w_ref, b_ref, o_ref):
    c = c_ref[...]
    o_ref[0] = jnp.dot(_silu(c), w_ref[0], precision=HIGHEST,
                       preferred_element_type=F32) + b_ref[0]


def _mod(cond8, mod_w, mod_b):
    depth, d, d6 = mod_w.shape
    return pl.pallas_call(
        _mod_body,
        grid=(depth, d6 // d),
        in_specs=[pl.BlockSpec((8, d), lambda l, j: (0, 0)),
                  pl.BlockSpec((1, d, d), lambda l, j: (l, 0, j)),
                  pl.BlockSpec((1, 1, d), lambda l, j: (l, 0, j))],
        out_specs=pl.BlockSpec((1, 8, d), lambda l, j: (l, 0, j)),
        out_shape=jax.ShapeDtypeStruct((depth, 8, d6), F32),
        compiler_params=_cparams(("arbitrary", "arbitrary")),
        name="mod",
    )(cond8, mod_w, mod_b.reshape(depth, 1, d6))


def _post(lat, y, m, gffn, rww_ref, rwh_ref, n_exp, lat_o, hx_o, aff_o):
    d = lat.shape[-1]
    lat_new = lat + m[2:3] * y
    h = _prenorm(lat_new, gffn, m[3:4], m[4:5])
    h_hi, h_lo = _split_bf16(h)
    wide = jnp.dot(h_hi, rww_ref[...], preferred_element_type=F32)
    logits = (wide[:, :LANES] + wide[:, LANES:]
              + jnp.dot(h_lo, rwh_ref[...], preferred_element_type=F32))
    lane = lax.broadcasted_iota(I32, logits.shape, 1)
    lg = jnp.where(lane < n_exp, logits, -jnp.inf)
    ex = jnp.exp(lg - jnp.max(lg, axis=-1, keepdims=True))
    aff = ex / jnp.sum(ex, axis=-1, keepdims=True)
    a_hi, a_lo = _split_bf16(aff)
    lat_o[0] = lat_new
    hx_o[0, :, :d] = h_hi
    hx_o[0, :, d:d + LANES] = a_hi
    hx_o[0, :, d + LANES:] = a_lo
    aff_o[0] = aff.T[:n_exp]


def _post_specs(bn, t, d, n_exp, tt):
    specs = [pl.BlockSpec((1, tt, d), lambda b, i: (b, i, 0)),
             pl.BlockSpec((1, tt, d + 2 * LANES), lambda b, i: (b, i, 0)),
             pl.BlockSpec((1, n_exp, tt), lambda b, i: (b, 0, i))]
    shapes = [jax.ShapeDtypeStruct((bn, t, d), F32),
              jax.ShapeDtypeStruct((bn, t, d + 2 * LANES), BF16),
              jax.ShapeDtypeStruct((bn, n_exp, t), F32)]
    return specs, shapes


def _glu_body(x_ref, m_ref, g_ref, w1_ref, b1_ref, u_ref):
    d = x_ref.shape[-1]
    m = m_ref[0]
    h = _prenorm(x_ref[0], g_ref[...], m[0:1], m[1:2]).astype(BF16)
    a = jnp.dot(h, w1_ref[:, :d], preferred_element_type=F32) + b1_ref[:, :d]
    gate = jnp.dot(h, w1_ref[:, d:], preferred_element_type=F32) + b1_ref[:, d:]
    u_ref[0] = (a * jax.nn.sigmoid(gate)).astype(u_ref.dtype)


def _glu(x, m, mrow, g, w1, b1, tt):
    bn, t, d = x.shape
    return pl.pallas_call(
        _glu_body,
        grid=(bn, t // tt),
        in_specs=[pl.BlockSpec((1, tt, d), lambda b, i: (b, i, 0)),
                  pl.BlockSpec((1, 6, d), lambda b, i: (mrow(b), 0, 0)),
                  pl.BlockSpec((1, d), lambda b, i: (0, 0)),
                  pl.BlockSpec((d, 2 * d), lambda b, i: (0, 0)),
                  pl.BlockSpec((1, 2 * d), lambda b, i: (0, 0))],
        out_specs=pl.BlockSpec((1, tt, d), lambda b, i: (b, i, 0)),
        out_shape=jax.ShapeDtypeStruct((bn, t, d), BF16),
        compiler_params=_cparams(("arbitrary", "arbitrary")),
        name="glu",
    )(x, m, g, w1, b1)


def _conv_body(nt, kw, n_exp, up_ref, uc_ref, un_ref, x_ref, m_ref, dw_ref, dwb_ref, lng_ref,
               lnb_ref, w2_ref, b2_ref, gffn_ref, rww_ref, rwh_ref, lat_o, hx_o, aff_o,
               xs_ref, sh_ref, cv_ref):
    i = pl.program_id(1)
    tt, d = cv_ref.shape
    pad = (kw - 1) // 2
    xs_ref[0:HALO, :] = jnp.where(i > 0, up_ref[0].astype(F32), 0.0)
    xs_ref[HALO:HALO + tt, :] = uc_ref[0].astype(F32)
    xs_ref[HALO + tt:, :] = jnp.where(i < nt - 1, un_ref[0].astype(F32), 0.0)

    strip = min(tt, 128)
    span = sh_ref.shape[1]

    def chan_block(j, carry):
        c0 = pl.multiple_of(j * LANES, LANES)
        for s in range(SUBLANES):
            sh_ref[s] = xs_ref[s:s + span, pl.ds(c0, LANES)]
        for st in range(tt // strip):
            acc = jnp.zeros((strip, LANES), F32)
            for k in range(kw):
                off = k + HALO - pad
                r0 = st * strip + (off // SUBLANES) * SUBLANES
                acc = acc + sh_ref[off % SUBLANES, r0:r0 + strip, :] * dw_ref[k:k + 1, pl.ds(c0, LANES)]
            cv_ref[st * strip:(st + 1) * strip, pl.ds(c0, LANES)] = acc + dwb_ref[:, pl.ds(c0, LANES)]
        return carry

    lax.fori_loop(0, d // LANES, chan_block, 0)

    cv = cv_ref[...]
    mu = jnp.mean(cv, axis=-1, keepdims=True)
    xc = cv - mu
    var = jnp.mean(xc * xc, axis=-1, keepdims=True)
    y = xc * lax.rsqrt(var + EPS) * lng_ref[...] + lnb_ref[...]
    z = jnp.dot(_silu(y).astype(BF16), w2_ref[...], preferred_element_type=F32) + b2_ref[...]
    _post(x_ref[0], z, m_ref[0], gffn_ref[...], rww_ref, rwh_ref, n_exp, lat_o, hx_o, aff_o)


def _conv(u, x, m, mrow, dw, dwb, lng, lnb, w2, b2, gffn, rww, rwh, n_exp, tt):
    bn, t, d = x.shape
    nt = t // tt
    kw = dw.shape[0]
    assert (kw - 1) // 2 <= HALO
    hb = tt // HALO
    nhb = t // HALO
    row = lambda b, i: (0, 0)
    out_specs, out_shape = _post_specs(bn, t, d, n_exp, tt)
    return pl.pallas_call(
        functools.partial(_conv_body, nt, kw, n_exp),
        grid=(bn, nt),
        in_specs=[pl.BlockSpec((1, HALO, d), lambda b, i: (b, jnp.maximum(i * hb - 1, 0), 0)),
                  pl.BlockSpec((1, tt, d), lambda b, i: (b, i, 0)),
                  pl.BlockSpec((1, HALO, d), lambda b, i: (b, jnp.minimum((i + 1) * hb, nhb - 1), 0)),
                  pl.BlockSpec((1, tt, d), lambda b, i: (b, i, 0)),
                  pl.BlockSpec((1, 6, d), lambda b, i: (mrow(b), 0, 0)),
                  pl.BlockSpec((kw, d), row),
                  pl.BlockSpec((1, d), row), pl.BlockSpec((1, d), row), pl.BlockSpec((1, d), row),
                  pl.BlockSpec((d, d), row), pl.BlockSpec((1, d), row), pl.BlockSpec((1, d), row),
                  pl.BlockSpec((d, 2 * LANES), row), pl.BlockSpec((d, LANES), row)],
        out_specs=out_specs,
        out_shape=out_shape,
        scratch_shapes=[pltpu.VMEM((tt + 2 * HALO, d), F32),
                        pltpu.VMEM((SUBLANES, tt + 2 * HALO - SUBLANES, LANES), F32),
                        pltpu.VMEM((tt, d), F32)],
        compiler_params=_cparams(("arbitrary", "arbitrary")),
        name="conv",
    )(u, u, u, x, m, dw, dwb, lng, lnb, w2, b2, gffn, rww, rwh)


def _proj_body(layer, x_ref, m_ref, g_ref, w_ref, lbl_ref, q_o, kf_o, kb_o, lf_o, lb_o, v_o, g_o):
    d = x_ref.shape[-1]
    m = m_ref[0]
    h = _prenorm(x_ref[0], g_ref[...], m[0:1], m[1:2]).astype(BF16)
    lg = lbl_ref[...]
    ex = jnp.exp(lg - jnp.max(lg, axis=0, keepdims=True))
    soft = ex / jnp.sum(ex, axis=0, keepdims=True)
    lbd = jnp.zeros(soft.shape[1:], F32)
    for l in range(1, layer + 1):
        lbd = lbd + soft[l]

    def col(j):
        return jnp.dot(h, w_ref[:, j * d:(j + 1) * d], preferred_element_type=F32)

    q_o[0] = col(0).astype(q_o.dtype)
    for dr, (k_o, l_o) in enumerate(((kf_o, lf_o), (kb_o, lb_o))):
        lb_row = lbd[dr:dr + 1]
        fg = lb_row + (1.0 - lb_row) * jax.nn.sigmoid(col(1 + dr))
        k_o[0] = (1.0 - fg).astype(k_o.dtype)
        l_o[0] = jnp.log(fg)
    v_o[0] = col(3).astype(v_o.dtype)
    g_o[0] = col(4).astype(g_o.dtype)


def _proj(x, m, mrow, g, w_in, lb_logits, layer, tt):
    bn, t, d = x.shape
    blk = pl.BlockSpec((1, tt, d), lambda b, i: (b, i, 0))
    sd = lambda dt: jax.ShapeDtypeStruct((bn, t, d), dt)
    return pl.pallas_call(
        functools.partial(_proj_body, layer),
        grid=(bn, t // tt),
        in_specs=[blk,
                  pl.BlockSpec((1, 6, d), lambda b, i: (mrow(b), 0, 0)),
                  pl.BlockSpec((1, d), lambda b, i: (0, 0)),
                  pl.BlockSpec((d, 5 * d), lambda b, i: (0, 0)),
                  pl.BlockSpec(lb_logits.shape, lambda b, i: (0, 0, 0))],
        out_specs=[blk] * 7,
        out_shape=[sd(BF16), sd(BF16), sd(BF16), sd(F32), sd(F32), sd(BF16), sd(BF16)],
        compiler_params=_cparams(("arbitrary", "arbitrary")),
        name="hgproj",
    )(x, m, g, w_in, lb_logits)


def _scan_chunk(reverse, q_ref, k_ref, v_ref, lf_ref, o_ref, st_ref, tmp_ref, rows):
    d = q_ref.shape[2]
    r = lax.broadcasted_iota(I32, (CHUNK, CHUNK), 0)
    c = lax.broadcasted_iota(I32, (CHUNK, CHUNK), 1)
    tri = ((c >= r) if reverse else (c <= r)).astype(BF16)
    rh = lax.broadcasted_iota(I32, (HALF, HALF), 0)
    ch = lax.broadcasted_iota(I32, (HALF, HALF), 1)
    keep_half = (ch >= rh) if reverse else (ch <= rh)
    tot_row = 0 if reverse else CHUNK - 1

    lf = lf_ref[0, rows, :]
    lf_hi, lf_lo = _split_bf16(lf)
    g = (jnp.dot(tri, lf_hi, preferred_element_type=F32)
         + jnp.dot(tri, lf_lo, preferred_element_type=F32))
    gtot = g[tot_row:tot_row + 1]
    q = q_ref[0, rows, :].astype(F32)
    k = k_ref[0, rows, :].astype(F32)
    v = v_ref[0, rows, :]
    q_in = (q * jnp.exp(g)).astype(BF16)
    k_st = (k * jnp.exp(gtot - g)).astype(BF16)
    d_st = jnp.exp(gtot)

    halves = (slice(0, HALF), slice(HALF, CHUNK))

    def max_abs(x):
        return jnp.max(jnp.max(jnp.abs(x), axis=0, keepdims=True), axis=1, keepdims=True)[0, 0]

    gmid = g[HALF:HALF + 1]
    half_refs = [g[hs][HALF // 2:HALF // 2 + 1] for hs in halves]
    whole_ok = max_abs(g - gmid) <= EXP_RANGE
    halves_ok = jnp.maximum(max_abs(g[halves[0]] - half_refs[0]),
                            max_abs(g[halves[1]] - half_refs[1])) <= EXP_RANGE

    def inter_and_state(h, sl):
        st = st_ref[h]
        o = lax.dot_general(q_in[:, sl], st.astype(BF16), CONTRACT_LAST, preferred_element_type=F32)
        st_ref[h] = st * d_st[:, sl] + lax.dot_general(
            v[:, sl], k_st[:, sl], CONTRACT_FIRST, preferred_element_type=F32)
        return o

    @pl.when(whole_ok)
    def _():
        keep = (c >= r) if reverse else (c <= r)
        q_mid = (q * jnp.exp(g - gmid)).astype(BF16)
        k_mid = (k * jnp.exp(gmid - g)).astype(BF16)
        for h in range(d // HEAD):
            sl = slice(h * HEAD, (h + 1) * HEAD)
            sc = lax.dot_general(q_mid[:, sl], k_mid[:, sl], CONTRACT_LAST,
                                 preferred_element_type=F32)
            sc = jnp.where(keep, sc, 0.0).astype(BF16)
            o_ref[0, rows, sl] = (inter_and_state(h, sl)
                                  + jnp.dot(sc, v[:, sl], preferred_element_type=F32))

    @pl.when(jnp.logical_not(whole_ok) & halves_ok)
    def _():
        q_dg = [(q[hs] * jnp.exp(g[hs] - ref)).astype(BF16) for hs, ref in zip(halves, half_refs)]
        k_dg = [(k[hs] * jnp.exp(ref - g[hs])).astype(BF16) for hs, ref in zip(halves, half_refs)]
        qs, ks = (halves[0], halves[1]) if reverse else (halves[1], halves[0])
        edge = g[HALF:HALF + 1] if reverse else g[HALF - 1:HALF]
        q_x = (q[qs] * jnp.exp(g[qs] - edge)).astype(BF16)
        k_x = (k[ks] * jnp.exp(edge - g[ks])).astype(BF16)
        seen = 1 if reverse else 0
        for h in range(d // HEAD):
            sl = slice(h * HEAD, (h + 1) * HEAD)
            diag = [jnp.where(keep_half,
                              lax.dot_general(q_dg[i][:, sl], k_dg[i][:, sl], CONTRACT_LAST,
                                              preferred_element_type=F32), 0.0) for i in range(2)]
            cross = lax.dot_general(q_x[:, sl], k_x[:, sl], CONTRACT_LAST,
                                    preferred_element_type=F32)
            vh = [v[hs, sl] for hs in halves]
            intra = [jnp.dot(diag[i].astype(BF16), vh[i], preferred_element_type=F32)
                     for i in range(2)]
            intra[1 - seen] = intra[1 - seen] + jnp.dot(cross.astype(BF16), vh[seen],
                                                        preferred_element_type=F32)
            o = inter_and_state(h, sl)
            for i, hs in enumerate(halves):
                o_ref[0, slice(rows.start + hs.start, rows.start + hs.stop), sl] = o[hs] + intra[i]

    @pl.when(jnp.logical_not(whole_ok | halves_ok))
    def _():
        row_id = lax.broadcasted_iota(I32, (ROW_ALIGN, d), 0)
        ngroup = CHUNK // ROW_ALIGN

        def group(gi, carry):
            gidx = (ngroup - 1 - gi) if reverse else gi
            r0 = pl.multiple_of(rows.start + gidx * ROW_ALIGN, ROW_ALIGN)
            grp = pl.ds(r0, ROW_ALIGN)
            tmp_ref[0] = q_ref[0, grp, :].astype(F32)
            tmp_ref[1] = k_ref[0, grp, :].astype(F32)
            tmp_ref[2] = v_ref[0, grp, :].astype(F32)
            tmp_ref[3] = jnp.exp(lf_ref[0, grp, :])
            tmp_ref[4] = jnp.zeros((ROW_ALIGN, d), F32)

            def token(ti, carry2):
                t = (ROW_ALIGN - 1 - ti) if reverse else ti
                spread_row = lambda j: jnp.broadcast_to(tmp_ref[j, pl.ds(t, 1), :], (ROW_ALIGN, d))
                only_first = lambda a: jnp.where(row_id == 0, a, 0.0).astype(BF16)
                q_all = spread_row(0).astype(BF16)
                k_one, v_one = only_first(spread_row(1)), only_first(spread_row(2))
                f_row = tmp_ref[3, pl.ds(t, 1), :]
                pieces = []
                for h in range(d // HEAD):
                    sl = slice(h * HEAD, (h + 1) * HEAD)
                    st = st_ref[h] * f_row[:, sl] + lax.dot_general(
                        v_one[:, sl], k_one[:, sl], CONTRACT_FIRST, preferred_element_type=F32)
                    st_ref[h] = st
                    pieces.append(lax.dot_general(q_all[:, sl], st.astype(BF16), CONTRACT_LAST,
                                                  preferred_element_type=F32))
                tmp_ref[4] = jnp.where(row_id == t, jnp.concatenate(pieces, axis=1), tmp_ref[4])
                return carry2

            lax.fori_loop(0, ROW_ALIGN, token, 0)
            o_ref[0, grp, :] = tmp_ref[4]
            return carry

        lax.fori_loop(0, ngroup, group, 0)


def _scan_body(nsteps, qf_ref, kf_ref, vf_ref, lf_ref, qb_ref, kb_ref, vb_ref, lb_ref, s0f_ref,
               s0b_ref, of_ref, ob_ref, sff_ref, sfb_ref, stf_ref, stb_ref, tmp_ref):
    step = pl.program_id(1)
    nchunk = qf_ref.shape[1] // CHUNK

    @pl.when(step == 0)
    def _():
        stf_ref[...] = s0f_ref[0]
        stb_ref[...] = s0b_ref[0]

    for ci in range(nchunk):
        cf, cb = ci, nchunk - 1 - ci
        _scan_chunk(False, qf_ref, kf_ref, vf_ref, lf_ref, of_ref, stf_ref, tmp_ref,
                    slice(cf * CHUNK, (cf + 1) * CHUNK))
        _scan_chunk(True, qb_ref, kb_ref, vb_ref, lb_ref, ob_ref, stb_ref, tmp_ref,
                    slice(cb * CHUNK, (cb + 1) * CHUNK))

    @pl.when(step == nsteps - 1)
    def _():
        sff_ref[0] = stf_ref[...]
        sfb_ref[0] = stb_ref[...]


def _scan(q, kf, kb, v, lf, lb, s0f, s0b, tq):
    bn, t, d = q.shape
    nh = d // HEAD
    nsteps = t // tq
    fblk = pl.BlockSpec((1, tq, d), lambda b, i: (b, i, 0))
    bblk = pl.BlockSpec((1, tq, d), lambda b, i: (b, nsteps - 1 - i, 0))
    sblk = pl.BlockSpec((1, nh, HEAD, HEAD), lambda b, i: (b, 0, 0, 0))
    st_shape = jax.ShapeDtypeStruct((bn, nh, HEAD, HEAD), F32)
    o_shape = jax.ShapeDtypeStruct((bn, t, d), F32)
    return pl.pallas_call(
        functools.partial(_scan_body, nsteps),
        grid=(bn, nsteps),
        in_specs=[fblk] * 4 + [bblk] * 4 + [sblk, sblk],
        out_specs=[fblk, bblk, sblk, sblk],
        out_shape=[o_shape, o_shape, st_shape, st_shape],
        scratch_shapes=[pltpu.VMEM((nh, HEAD, HEAD), F32), pltpu.VMEM((nh, HEAD, HEAD), F32),
                        pltpu.VMEM((5, ROW_ALIGN, d), F32)],
        compiler_params=_cparams(("arbitrary", "arbitrary")),
        name="hgscan",
    )(q, kf, v, lf, q, kb, v, lb, s0f, s0b)


def _hgpost_body(n_exp, of_ref, ob_ref, g_ref, x_ref, m_ref, ng_ref, wo_ref, gffn_ref, rww_ref,
                 rwh_ref, lat_o, hx_o, aff_o):
    o = of_ref[0] + ob_ref[0]
    d = o.shape[-1]
    parts = []
    for h in range(d // HEAD):
        oh = o[:, h * HEAD:(h + 1) * HEAD]
        ms = jnp.mean(oh * oh, axis=-1, keepdims=True)
        parts.append(oh * lax.rsqrt(ms + EPS))
    on = jnp.concatenate(parts, axis=1) * ng_ref[...]
    y = (on * _silu(g_ref[0].astype(F32))).astype(BF16)
    z = jnp.dot(y, wo_ref[...], preferred_element_type=F32)
    _post(x_ref[0], z, m_ref[0], gffn_ref[...], rww_ref, rwh_ref, n_exp, lat_o, hx_o, aff_o)


def _hgpost(o_fw, o_bw, g, x, m, mrow, ng_tiled, w_out, gffn, rww, rwh, n_exp, tt):
    bn, t, d = x.shape
    blk = pl.BlockSpec((1, tt, d), lambda b, i: (b, i, 0))
    row = lambda b, i: (0, 0)
    out_specs, out_shape = _post_specs(bn, t, d, n_exp, tt)
    return pl.pallas_call(
        functools.partial(_hgpost_body, n_exp),
        grid=(bn, t // tt),
        in_specs=[blk, blk, blk, blk,
                  pl.BlockSpec((1, 6, d), lambda b, i: (mrow(b), 0, 0)),
                  pl.BlockSpec((1, d), row), pl.BlockSpec((d, d), row), pl.BlockSpec((1, d), row),
                  pl.BlockSpec((d, 2 * LANES), row), pl.BlockSpec((d, LANES), row)],
        out_specs=out_specs,
        out_shape=out_shape,
        compiler_params=_cparams(("arbitrary", "arbitrary")),
        name="hgpost",
    )(o_fw, o_bw, g, x, m, ng_tiled, w_out, gffn, rww, rwh)


def _route_body(n_exp, nb, k_sel, a_ref, code_e_o, code_t_o, lo_o, code_s):
    rows = n_exp * nb
    shift = nb.bit_length() - 1
    a3 = a_ref[0].reshape(n_exp, nb, LANES)

    def count(mask3):
        s = jnp.sum(jnp.where(mask3, 1.0, 0.0), axis=1, keepdims=True)
        return jnp.sum(s, axis=2, keepdims=True)

    thr = jnp.zeros((n_exp, 1, 1), I32)
    for bit in range(30, -1, -1):
        cand = thr | (1 << bit)
        enough = count(a3 >= lax.bitcast_convert_type(cand, F32)) >= k_sel
        thr = jnp.where(enough, cand, thr)
    lo = lax.bitcast_convert_type(thr, F32)
    hi = lax.bitcast_convert_type(thr + 1, F32)
    for _ in range(REFINE_STEPS):
        mid = 0.5 * (lo + hi)
        enough = count(a3 >= mid) >= k_sel
        lo = jnp.where(enough, mid, lo)
        hi = jnp.where(enough, hi, mid)
    gt3 = a3 >= hi
    eq3 = (a3 >= lo) & jnp.logical_not(gt3)
    need = k_sel - count(gt3)

    ri = lax.broadcasted_iota(I32, (rows, rows), 0)
    ci = lax.broadcasted_iota(I32, (rows, rows), 1)
    lblk = (((ri >> shift) == (ci >> shift)) & (ci < ri)).astype(BF16)
    ui = lax.broadcasted_iota(I32, (LANES, LANES), 0)
    uj = lax.broadcasted_iota(I32, (LANES, LANES), 1)
    upper = (ui <= uj).astype(BF16)
    ones = jnp.ones((LANES, LANES), BF16)

    def cumsum(x):
        xb = x.astype(BF16)
        within = jnp.dot(xb, upper, preferred_element_type=F32)
        part = jnp.dot(lblk, xb, preferred_element_type=F32)
        before = jnp.dot(part.astype(BF16), ones, preferred_element_type=F32)
        return within + before, before

    eqf = jnp.where(eq3, 1.0, 0.0).reshape(rows, LANES)
    eq_incl, _ = cumsum(eqf)
    eq_excl3 = (eq_incl - eqf).reshape(n_exp, nb, LANES)
    sel3 = gt3 | (eq3 & (eq_excl3 < need))
    self = jnp.where(sel3, 1.0, 0.0).reshape(rows, LANES)
    incl, before = cumsum(self)
    lo_o[0] = before.astype(I32)
    code = jnp.where(self > 0.0, incl - self, -1.0).astype(I32)
    code_e_o[0] = code
    code_s[...] = code

    fill = jnp.full((LANES - n_exp, LANES), -1, I32)
    for blk in range(nb):
        tile = code_s[pl.ds(blk, n_exp, stride=nb), :]
        code_t_o[0, blk * LANES:(blk + 1) * LANES, :] = jnp.concatenate([tile, fill], axis=0).T


def _route(aff, k_sel):
    bn, n_exp, t = aff.shape
    tp = max(t, SUBLANES * LANES)
    if tp != t:
        aff = jnp.pad(aff, ((0, 0), (0, 0), (0, tp - t)), constant_values=-1.0)
    nb = tp // LANES
    assert nb & (nb - 1) == 0
    rows = n_exp * nb
    blk = pl.BlockSpec((1, rows, LANES), lambda b: (b, 0, 0))
    code_e, code_t, lo = pl.pallas_call(
        functools.partial(_route_body, n_exp, nb, k_sel),
        grid=(bn,),
        in_specs=[blk],
        out_specs=[blk, pl.BlockSpec((1, tp, LANES), lambda b: (b, 0, 0)), blk],
        out_shape=[jax.ShapeDtypeStruct((bn, rows, LANES), I32),
                   jax.ShapeDtypeStruct((bn, tp, LANES), I32),
                   jax.ShapeDtypeStruct((bn, rows, LANES), I32)],
        scratch_shapes=[pltpu.VMEM((rows, LANES), I32)],
        compiler_params=_cparams(("arbitrary",)),
        name="route",
    )(aff.reshape(bn, rows, LANES))
    lo3 = lo[:, :, 0].reshape(bn, n_exp, nb)
    lo_ext = jnp.concatenate([lo3, jnp.full((bn, n_exp, 1), k_sel, I32)], axis=-1)
    return code_e.reshape(bn, n_exp, tp), code_t, lo_ext.reshape(-1), nb + 1


def _dispatch_body(n_exp, nbp, cap, lo_ref, hx_ref, code_ref, x_ref, xwin, carry, pend, sem):
    b = pl.program_id(0)
    i = pl.program_id(1)
    tt = hx_ref.shape[1]
    bpt = tt // LANES
    base = b * cap
    is_last = (b == pl.num_programs(0) - 1) & (i == pl.num_programs(1) - 1)

    @pl.when(i == 0)
    def _():
        carry[...] = jnp.zeros_like(carry)

    def out_copy(e, row):
        return pltpu.make_async_copy(xwin.at[e], x_ref.at[e, pl.ds(row, WIN)], sem.at[0])

    def drain():
        for e in range(n_exp):
            @pl.when(pend[e] == 1)
            def _(e=e):
                out_copy(e, 0).wait()
                pend[e] = 0

    @pl.when((b == 0) & (i == 0))
    def _():
        xwin[...] = jnp.zeros_like(xwin)
        for e in range(n_exp):
            out_copy(e, pl.num_programs(0) * cap).start()
            pend[e] = 1

    starts, next_blk = [], []
    npass = jnp.int32(0)
    for e in range(n_exp):
        o = (b * n_exp + e) * nbp + i * bpt
        a_e = (lo_ref[o] >> ROW_SHIFT) << ROW_SHIFT
        c_e = ((lo_ref[o + bpt] >> ROW_SHIFT) << ROW_SHIFT) - a_e
        starts.append(a_e)
        next_blk.append(c_e)
        npass = jnp.maximum(npass, (c_e >> WIN_SHIFT) + 1)

    h = hx_ref[0]
    code = code_ref[0]
    jrow = lax.broadcasted_iota(I32, (WIN, tt), 0)

    def one_pass(p, loop_carry):
        onehot = []
        for e in range(n_exp):
            rel = code[e:e + 1, :] - (starts[e] + p * WIN)
            onehot.append(jnp.where(rel == jrow, 1.0, 0.0).astype(BF16))
        xw = jnp.dot(jnp.concatenate(onehot, axis=0), h, preferred_element_type=F32)
        drain()
        for e in range(n_exp):
            xwin[e] = xw[e * WIN:(e + 1) * WIN].astype(BF16)

        @pl.when(p == 0)
        def _():
            for e in range(n_exp):
                xwin[e, 0:ROW_ALIGN, :] = xwin[e, 0:ROW_ALIGN, :] + carry[e]

        for e in range(n_exp):
            off = next_blk[e] - p * WIN

            @pl.when((off >= 0) & (off < WIN))
            def _(e=e, off=off):
                carry[e] = xwin[e, pl.ds(pl.multiple_of(off, ROW_ALIGN), ROW_ALIGN), :]

            @pl.when(off >= 0)
            def _(e=e):
                out_copy(e, pl.multiple_of(base + starts[e] + p * WIN, ROW_ALIGN)).start()
                pend[e] = 1
        return loop_carry

    lax.fori_loop(0, npass, one_pass, 0)

    @pl.when(is_last)
    def _():
        drain()


def _dispatch(hx, code_e, lo_ext, nbp, cap, tt):
    bn, t, dx = hx.shape
    n_exp = code_e.shape[1]
    assert cap % ROW_ALIGN == 0
    grid_spec = pltpu.PrefetchScalarGridSpec(
        num_scalar_prefetch=1,
        grid=(bn, t // tt),
        in_specs=[pl.BlockSpec((1, tt, dx), lambda b, i, lo: (b, i, 0)),
                  pl.BlockSpec((1, n_exp, tt), lambda b, i, lo: (b, 0, i))],
        out_specs=pl.BlockSpec(memory_space=pl.ANY),
        scratch_shapes=[pltpu.VMEM((n_exp, WIN, dx), BF16), pltpu.VMEM((n_exp, ROW_ALIGN, dx), BF16),
                        pltpu.SMEM((n_exp,), I32), pltpu.SemaphoreType.DMA((1,))],
    )
    return pl.pallas_call(
        functools.partial(_dispatch_body, n_exp, nbp, cap),
        grid_spec=grid_spec,
        out_shape=jax.ShapeDtypeStruct((n_exp, bn * cap + WIN, dx), BF16),
        compiler_params=pltpu.CompilerParams(dimension_semantics=("arbitrary", "arbitrary"),
                                             vmem_limit_bytes=VMEM_LIMIT_BYTES,
                                             has_side_effects=True),
        name="dispatch",
    )(lo_ext, hx, code_e)


def _ffn_body(d, nf, with_ctx, *refs):
    if with_ctx:
        x_ref, xc_ref, wg_ref, wu_ref, wd_ref, y_ref, yc_ref, acc_ref, accc_ref = refs
    else:
        x_ref, wg_ref, wu_ref, wd_ref, y_ref, acc_ref = refs
    e = pl.program_id(0)
    mi = pl.program_id(1)
    f = pl.program_id(2)
    wg = wg_ref[0, 0].astype(BF16)
    wu = wu_ref[0, 0].astype(BF16)
    wd = wd_ref[0, 0].astype(BF16)

    def run(xr, yr, acc):
        x = xr[0, :, :d]
        a = jnp.dot(x, wg, preferred_element_type=F32)
        u = jnp.dot(x, wu, preferred_element_type=F32)
        part = jnp.dot((_silu(a) * u).astype(BF16), wd, preferred_element_type=F32)

        @pl.when(f == 0)
        def _():
            acc[...] = part

        @pl.when(f > 0)
        def _():
            acc[...] += part

        @pl.when(f == nf - 1)
        def _():
            aff = xr[0, :, d:d + LANES].astype(F32) + xr[0, :, d + LANES:].astype(F32)
            lane = lax.broadcasted_iota(I32, aff.shape, 1)
            gate = jnp.sum(jnp.where(lane == e, aff, 0.0), axis=-1, keepdims=True)
            yr[0] = (acc[...] * gate).astype(yr.dtype)

    run(x_ref, y_ref, acc_ref)
    if with_ctx:
        @pl.when(mi == 0)
        def _():
            run(xc_ref, yc_ref, accc_ref)


def _ffn(xl, xc, w_gate, w_up, w_down, layer, tm, tmc, fc):
    n_exp, _, dx = xl.shape
    d = dx - 2 * LANES
    nm = (xl.shape[1] - WIN) // tm
    nf = w_gate.shape[-1] // fc
    with_ctx = xc is not None
    in_specs = [pl.BlockSpec((1, tm, dx), lambda e, mi, f: (e, mi, 0))]
    out_specs = [pl.BlockSpec((1, tm, d), lambda e, mi, f: (e, mi, 0))]
    out_shape = [jax.ShapeDtypeStruct((n_exp, nm * tm, d), BF16)]
    scratch = [pltpu.VMEM((tm, d), F32)]
    args = [xl]
    if with_ctx:
        in_specs.append(pl.BlockSpec((1, tmc, dx), lambda e, mi, f: (e, 0, 0)))
        out_specs.append(pl.BlockSpec((1, tmc, d), lambda e, mi, f: (e, 0, 0)))
        out_shape.append(jax.ShapeDtypeStruct((n_exp, tmc, d), BF16))
        scratch.append(pltpu.VMEM((tmc, d), F32))
        args.append(xc)
    in_specs += [pl.BlockSpec((1, 1, d, fc), lambda e, mi, f: (layer, e, 0, f)),
                 pl.BlockSpec((1, 1, d, fc), lambda e, mi, f: (layer, e, 0, f)),
                 pl.BlockSpec((1, 1, fc, d), lambda e, mi, f: (layer, e, f, 0))]
    outs = pl.pallas_call(
        functools.partial(_ffn_body, d, nf, with_ctx),
        grid=(n_exp, nm, nf),
        in_specs=in_specs,
        out_specs=out_specs,
        out_shape=out_shape,
        scratch_shapes=scratch,
        compiler_params=_cparams(("arbitrary", "arbitrary", "arbitrary")),
        name="ffn",
    )(*args, w_gate, w_up, w_down)
    return (outs[0], outs[1], nm * tm) if with_ctx else (outs[0], None, nm * tm)


def _combine_body(n_exp, nbp, cap, mtot, final, lo_ref, y_ref, x_ref, code_ref, m_ref,
                  gfin_ref, sel_ref, out_ref, ybuf, yextra, acc_ref, sem):
    b = pl.program_id(0)
    i = pl.program_id(1)
    nt = pl.num_programs(1)
    step = b * nt + i
    slot = step % 2
    tt, d = acc_ref.shape
    bpt = tt // LANES
    base = b * cap

    def tile_windows(bb, ii):
        firsts, npass = [], jnp.int32(0)
        for e in range(n_exp):
            o = (bb * n_exp + e) * nbp + ii * bpt
            a_e = bb * cap + ((lo_ref[o] >> ROW_SHIFT) << ROW_SHIFT)
            firsts.append(a_e)
            npass = jnp.maximum(npass, (bb * cap + lo_ref[o + bpt] - a_e + WIN - 1) >> WIN_SHIFT)
        return firsts, npass

    def clamp(w):
        return pl.multiple_of(jnp.minimum(w, mtot - WIN), ROW_ALIGN)

    def win_copy(e, row, buf, k):
        return pltpu.make_async_copy(y_ref.at[e, pl.ds(row, WIN)], buf.at[e], sem.at[k])

    starts, npass = tile_windows(b, i)

    @pl.when(step == 0)
    def _():
        for e in range(n_exp):
            win_copy(e, clamp(starts[e]), ybuf.at[0], 0).start()

    @pl.when(step + 1 < pl.num_programs(0) * nt)
    def _():
        wrap = i + 1 == nt
        nxt, _ = tile_windows(jnp.where(wrap, b + 1, b), jnp.where(wrap, 0, i + 1))
        for e in range(n_exp):
            win_copy(e, clamp(nxt[e]), ybuf.at[1 - slot], 1 - slot).start()

    acc_ref[...] = jnp.zeros_like(acc_ref)
    c1 = code_ref[0] + 1
    c_hi = (c1 >> CODE_SPLIT_SHIFT).astype(F32).astype(BF16)
    c_lo = (c1 & ((1 << CODE_SPLIT_SHIFT) - 1)).astype(F32).astype(BF16)
    spread = (jnp.dot(c_hi, sel_ref[...], preferred_element_type=F32) * float(1 << CODE_SPLIT_SHIFT)
              + jnp.dot(c_lo, sel_ref[...], preferred_element_type=F32))
    lane = lax.broadcasted_iota(I32, (1, n_exp * WIN), 1)
    lane_e = lane >> WIN_SHIFT
    lane_j = lane & (WIN - 1)

    def per_expert_row(vals):
        row = jnp.zeros_like(lane)
        for e in range(n_exp):
            row = jnp.where(lane_e == e, vals[e], row)
        return row

    def expand(buf, want, rows):
        slot_rank = (per_expert_row([r - base for r in rows]) + lane_j + 1).astype(F32)
        first_rank = (per_expert_row([w - base for w in want]) + 1).astype(F32)
        hit = (spread == slot_rank) & (spread >= first_rank)
        p_mat = jnp.where(hit, 1.0, 0.0).astype(BF16)
        acc_ref[...] += jnp.dot(p_mat, buf[...].reshape(n_exp * WIN, d), preferred_element_type=F32)

    rows0 = [clamp(a) for a in starts]
    for e in range(n_exp):
        win_copy(e, rows0[e], ybuf.at[slot], slot).wait()
    expand(ybuf.at[slot], starts, rows0)

    def later_pass(p, carry):
        want = [a + p * WIN for a in starts]
        rows = [clamp(w) for w in want]
        for e in range(n_exp):
            win_copy(e, rows[e], yextra, 2).start()
        for e in range(n_exp):
            win_copy(e, rows[e], yextra, 2).wait()
        expand(yextra, want, rows)
        return carry

    lax.fori_loop(1, npass, later_pass, 0)

    m = m_ref[0]
    lat_new = x_ref[0] + m[5:6] * acc_ref[...]
    if final:
        out_ref[0] = _rms(lat_new, gfin_ref[...])
    else:
        out_ref[0] = lat_new


def _combine(y, mtot, x, code_t, lo_ext, nbp, m, mrow, gfin, cap, final, tt):
    bn, t, d = x.shape
    n_exp = y.shape[0]
    assert cap >> CODE_SPLIT_SHIFT <= 256
    sel = (jnp.arange(LANES)[:, None] == jnp.arange(n_exp * WIN)[None, :] // WIN).astype(BF16)
    grid_spec = pltpu.PrefetchScalarGridSpec(
        num_scalar_prefetch=1,
        grid=(bn, t // tt),
        in_specs=[pl.BlockSpec(memory_space=pl.ANY),
                  pl.BlockSpec((1, tt, d), lambda b, i, lo: (b, i, 0)),
                  pl.BlockSpec((1, tt, LANES), lambda b, i, lo: (b, i, 0)),
                  pl.BlockSpec((1, 6, d), lambda b, i, lo: (mrow(b), 0, 0)),
                  pl.BlockSpec((1, d), lambda b, i, lo: (0, 0)),
                  pl.BlockSpec((LANES, n_exp * WIN), lambda b, i, lo: (0, 0))],
        out_specs=pl.BlockSpec((1, tt, d), lambda b, i, lo: (b, i, 0)),
        scratch_shapes=[pltpu.VMEM((2, n_exp, WIN, d), BF16), pltpu.VMEM((n_exp, WIN, d), BF16),
                        pltpu.VMEM((tt, d), F32), pltpu.SemaphoreType.DMA((3,))],
    )
    return pl.pallas_call(
        functools.partial(_combine_body, n_exp, nbp, cap, mtot, final),
        grid_spec=grid_spec,
        out_shape=jax.ShapeDtypeStruct((bn, t, d), F32),
        compiler_params=_cparams(("arbitrary", "arbitrary")),
        name="combine",
    )(lo_ext, y, x, code_t, m, gfin, sel)


def _moe(hx_l, aff_l, lat_mid, hx_c, aff_c, ctx_mid, m, layer, wg, wu, wd, final_g, last, tt_l, tt_c):
    bn, t, _ = hx_l.shape
    n_exp = aff_l.shape[1]
    with_ctx = hx_c is not None
    cap_l = CAPACITY_FACTOR * t // n_exp
    code_e_l, code_t_l, lo_l, nbp_l = _route(aff_l, cap_l)
    xl = _dispatch(hx_l, code_e_l, lo_l, nbp_l, cap_l, tt_l)
    xc, cap_c = None, 0
    if with_ctx:
        cap_c = CAPACITY_FACTOR * hx_c.shape[1] // n_exp
        code_e_c, code_t_c, lo_c, nbp_c = _route(aff_c, cap_c)
        xc = _dispatch(hx_c, code_e_c, lo_c, nbp_c, cap_c, tt_c)
    fc = min(512, wg.shape[-1])
    yl, yc, rows_l = _ffn(xl, xc, wg, wu, wd, layer, cap_l, bn * cap_c, fc)
    lat_new = _combine(yl, rows_l, lat_mid, code_t_l, lo_l, nbp_l, m, lambda b: layer * 8 + b,
                       final_g, cap_l, last, tt_l)
    ctx_new = None
    if with_ctx:
        ctx_new = _combine(yc, bn * cap_c, ctx_mid, code_t_c, lo_c, nbp_c, m,
                           lambda b: layer * 8 + bn, final_g, cap_c, False, tt_c)
    return lat_new, ctx_new


def kernel(x, c, ctx, c_ctx, mod_w, mod_b, norm_mix_g, norm_ffn_g, cv_w1, cv_b1, cv_dw, cv_dwb,
           cv_ln_g, cv_ln_b, cv_w2, cv_b2, hg_w_in, hg_lb_logits, hg_norm_g, hg_w_out,
           router_w, exp_w_gate, exp_w_up, exp_w_down, final_g):
    bn, t, d = x.shape
    tc = ctx.shape[1]
    depth = mod_w.shape[0]
    n_exp = router_w.shape[-1]
    assert bn < 8 and d % HEAD == 0 and n_exp % 4 == 0 and n_exp <= LANES

    cond8 = jnp.concatenate([c, c_ctx[None], jnp.zeros((8 - bn - 1, d), F32)], axis=0)
    m = _mod(cond8, mod_w, mod_b).reshape(depth * 8, 6, d)

    tt_l = min(512, t)
    tt_c = min(256, tc)
    tmoe_l = min(256, t)
    tmoe_c = min(256, tc)
    row = lambda a: a.reshape(1, -1)
    fin = row(final_g)

    lat, cx = x, ctx
    for i in range(depth):
        last = i == depth - 1
        j = i // 2
        lat_row = lambda b, i=i: i * 8 + b
        ctx_row = lambda b, i=i: i * 8 + bn
        gmix, gffn = row(norm_mix_g[i]), row(norm_ffn_g[i])
        rw_pad = jnp.pad(router_w[i], ((0, 0), (0, LANES - n_exp)))
        rwh = rw_pad.astype(BF16)
        rww = jnp.concatenate([rwh, (rw_pad - rwh.astype(F32)).astype(BF16)], axis=1)
        need_ctx = not last
        hx_c = aff_c = ctx_mid = None
        if i % 2 == 0:
            w1 = cv_w1[j].astype(BF16)
            w2 = cv_w2[j].astype(BF16)
            cargs = (cv_dw[j], row(cv_dwb[j]), row(cv_ln_g[j]), row(cv_ln_b[j]), w2, row(cv_b2[j]),
                     gffn, rww, rwh, n_exp)
            u = _glu(lat, m, lat_row, gmix, w1, row(cv_b1[j]), tt_l)
            lat_mid, hx_l, aff_l = _conv(u, lat, m, lat_row, *cargs, tt_l)
            if need_ctx:
                uc = _glu(cx, m, ctx_row, gmix, w1, row(cv_b1[j]), tt_c)
                ctx_mid, hx_c, aff_c = _conv(uc, cx, m, ctx_row, *cargs, tt_c)
        else:
            w_in = hg_w_in[j].astype(BF16)
            w_out = hg_w_out[j].astype(BF16)
            ng = row(jnp.tile(hg_norm_g[j], d // HEAD))
            s_zero = jnp.zeros((bn, d // HEAD, HEAD, HEAD), F32)
            qc, kfc, kbc, lfc, lbc, vc, gc = _proj(cx, m, ctx_row, gmix, w_in, hg_lb_logits, i, tt_c)
            ql, kfl, kbl, lfl, lbl, vl, gl = _proj(lat, m, lat_row, gmix, w_in, hg_lb_logits, i, tt_l)
            tq_c, tq_l = min(256, tc), min(256, t)
            oc_f, oc_b, sc_f, sc_b = _scan(qc, kfc, kbc, vc, lfc, lbc, s_zero, s_zero, tq_c)
            ol_f, ol_b, _, _ = _scan(ql, kfl, kbl, vl, lfl, lbl, sc_f, sc_b, tq_l)
            pargs = (ng, w_out, gffn, rww, rwh, n_exp)
            lat_mid, hx_l, aff_l = _hgpost(ol_f, ol_b, gl, lat, m, lat_row, *pargs, tt_l)
            if need_ctx:
                ctx_mid, hx_c, aff_c = _hgpost(oc_f, oc_b, gc, cx, m, ctx_row, *pargs, tt_c)
        lat, cx_new = _moe(hx_l, aff_l, lat_mid, hx_c, aff_c, ctx_mid, m, i, exp_w_gate, exp_w_up,
                           exp_w_down, fin, last, tmoe_l, tmoe_c)
        if need_ctx:
            cx = cx_new
    return lat
```

```python
import functools

import jax
import jax.numpy as jnp
from jax import lax
from jax.experimental import pallas as pl
from jax.experimental.pallas import tpu as pltpu

F32 = jnp.float32
BF16 = jnp.bfloat16
I32 = jnp.int32
EPS = 1e-6
HIGHEST = lax.Precision.HIGHEST

LANES = 128
SUBLANES = 8
HEAD = 128
CHUNK = 128
HALF = CHUNK // 2
EXP_RANGE = 80.0
CAPACITY_FACTOR = 2
HALO = 16
WIN = 64
WIN_SHIFT = 6
ROW_ALIGN = 16
ROW_SHIFT = 4
REFINE_STEPS = 26
CODE_SPLIT_SHIFT = 5
VMEM_LIMIT_BYTES = 56 * 1024 * 1024

CONTRACT_LAST = (((1,), (1,)), ((), ()))
CONTRACT_FIRST = (((0,), (0,)), ((), ()))


def _cparams(sem):
    return pltpu.CompilerParams(dimension_semantics=sem, vmem_limit_bytes=VMEM_LIMIT_BYTES)


def _silu(x):
    return x * jax.nn.sigmoid(x)


def _rms(x, g):
    ms = jnp.mean(x * x, axis=-1, keepdims=True)
    return x * lax.rsqrt(ms + EPS) * g


def _prenorm(x, g, shift, scale):
    return _rms(x, g) * (1.0 + scale) + shift


def _split_bf16(x):
    hi = x.astype(BF16)
    return hi, (x - hi.astype(F32)).astype(BF16)


def _mod_body(c_ref, w_ref, b_ref, o_ref):
    c = c_ref[...]
    o_ref[0] = jnp.dot(_silu(c), w_ref[0], precision=HIGHEST,
                       preferred_element_type=F32) + b_ref[0]


def _mod(cond8, mod_w, mod_b):
    depth, d, d6 = mod_w.shape
    return pl.pallas_call(
        _mod_body,
        grid=(depth, d6 // d),
        in_specs=[pl.BlockSpec((8, d), lambda l, j: (0, 0)),
                  pl.BlockSpec((1, d, d), lambda l, j: (l, 0, j)),
                  pl.BlockSpec((1, 1, d), lambda l, j: (l, 0, j))],
        out_specs=pl.BlockSpec((1, 8, d), lambda l, j: (l, 0, j)),
        out_shape=jax.ShapeDtypeStruct((depth, 8, d6), F32),
        compiler_params=_cparams(("arbitrary", "arbitrary")),
        name="mod",
    )(cond8, mod_w, mod_b.reshape(depth, 1, d6))


def _post(lat, y, m, gffn, rww_ref, rwh_ref, n_exp, lat_o, hx_o, aff_o):
    d = lat.shape[-1]
    lat_new = lat + m[2:3] * y
    h = _prenorm(lat_new, gffn, m[3:4], m[4:5])
    h_hi, h_lo = _split_bf16(h)
    wide = jnp.dot(h_hi, rww_ref[...], preferred_element_type=F32)
    logits = (wide[:, :LANES] + wide[:, LANES:]
              + jnp.dot(h_lo, rwh_ref[...], preferred_element_type=F32))
    lane = lax.broadcasted_iota(I32, logits.shape, 1)
    lg = jnp.where(lane < n_exp, logits, -jnp.inf)
    ex = jnp.exp(lg - jnp.max(lg, axis=-1, keepdims=True))
    aff = ex / jnp.sum(ex, axis=-1, keepdims=True)
    a_hi, a_lo = _split_bf16(aff)
    lat_o[0] = lat_new
    hx_o[0, :, :d] = h_hi
    hx_o[0, :, d:d + LANES] = a_hi
    hx_o[0, :, d + LANES:] = a_lo
    aff_o[0] = aff.T[:n_exp]


def _post_specs(bn, t, d, n_exp, tt):
    specs = [pl.BlockSpec((1, tt, d), lambda b, i: (b, i, 0)),
             pl.BlockSpec((1, tt, d + 2 * LANES), lambda b, i: (b, i, 0)),
             pl.BlockSpec((1, n_exp, tt), lambda b, i: (b, 0, i))]
    shapes = [jax.ShapeDtypeStruct((bn, t, d), F32),
              jax.ShapeDtypeStruct((bn, t, d + 2 * LANES), BF16),
              jax.ShapeDtypeStruct((bn, n_exp, t), F32)]
    return specs, shapes


def _glu_body(x_ref, m_ref, g_ref, w1_ref, b1_ref, u_ref):
    d = x_ref.shape[-1]
    m = m_ref[0]
    h = _prenorm(x_ref[0], g_ref[...], m[0:1], m[1:2]).astype(BF16)
    a = jnp.dot(h, w1_ref[:, :d], preferred_element_type=F32) + b1_ref[:, :d]
    gate = jnp.dot(h, w1_ref[:, d:], preferred_element_type=F32) + b1_ref[:, d:]
    u_ref[0] = (a * jax.nn.sigmoid(gate)).astype(u_ref.dtype)


def _glu(x, m, mrow, g, w1, b1, tt):
    bn, t, d = x.shape
    return pl.pallas_call(
        _glu_body,
        grid=(bn, t // tt),
        in_specs=[pl.BlockSpec((1, tt, d), lambda b, i: (b, i, 0)),
                  pl.BlockSpec((1, 6, d), lambda b, i: (mrow(b), 0, 0)),
                  pl.BlockSpec((1, d), lambda b, i: (0, 0)),
                  pl.BlockSpec((d, 2 * d), lambda b, i: (0, 0)),
                  pl.BlockSpec((1, 2 * d), lambda b, i: (0, 0))],
        out_specs=pl.BlockSpec((1, tt, d), lambda b, i: (b, i, 0)),
        out_shape=jax.ShapeDtypeStruct((bn, t, d), BF16),
        compiler_params=_cparams(("arbitrary", "arbitrary")),
        name="glu",
    )(x, m, g, w1, b1)


def _conv_body(nt, kw, n_exp, up_ref, uc_ref, un_ref, x_ref, m_ref, dw_ref, dwb_ref, lng_ref,
               lnb_ref, w2_ref, b2_ref, gffn_ref, rww_ref, rwh_ref, lat_o, hx_o, aff_o,
               xs_ref, sh_ref, cv_ref):
    i = pl.program_id(1)
    tt, d = cv_ref.shape
    pad = (kw - 1) // 2
    xs_ref[0:HALO, :] = jnp.where(i > 0, up_ref[0].astype(F32), 0.0)
    xs_ref[HALO:HALO + tt, :] = uc_ref[0].astype(F32)
    xs_ref[HALO + tt:, :] = jnp.where(i < nt - 1, un_ref[0].astype(F32), 0.0)

    strip = min(tt, 128)
    span = sh_ref.shape[1]

    def chan_block(j, carry):
        c0 = pl.multiple_of(j * LANES, LANES)
        for s in range(SUBLANES):
            sh_ref[s] = xs_ref[s:s + span, pl.ds(c0, LANES)]
        for st in range(tt // strip):
            acc = jnp.zeros((strip, LANES), F32)
            for k in range(kw):
                off = k + HALO - pad
                r0 = st * strip + (off // SUBLANES) * SUBLANES
                acc = acc + sh_ref[off % SUBLANES, r0:r0 + strip, :] * dw_ref[k:k + 1, pl.ds(c0, LANES)]
            cv_ref[st * strip:(st + 1) * strip, pl.ds(c0, LANES)] = acc + dwb_ref[:, pl.ds(c0, LANES)]
        return carry

    lax.fori_loop(0, d // LANES, chan_block, 0)

    cv = cv_ref[...]
    mu = jnp.mean(cv, axis=-1, keepdims=True)
    xc = cv - mu
    var = jnp.mean(xc * xc, axis=-1, keepdims=True)
    y = xc * lax.rsqrt(var + EPS) * lng_ref[...] + lnb_ref[...]
    z = jnp.dot(_silu(y).astype(BF16), w2_ref[...], preferred_element_type=F32) + b2_ref[...]
    _post(x_ref[0], z, m_ref[0], gffn_ref[...], rww_ref, rwh_ref, n_exp, lat_o, hx_o, aff_o)


def _conv(u, x, m, mrow, dw, dwb, lng, lnb, w2, b2, gffn, rww, rwh, n_exp, tt):
    bn, t, d = x.shape
    nt = t // tt
    kw = dw.shape[0]
    assert (kw - 1) // 2 <= HALO
    hb = tt // HALO
    nhb = t // HALO
    row = lambda b, i: (0, 0)
    out_specs, out_shape = _post_specs(bn, t, d, n_exp, tt)
    return pl.pallas_call(
        functools.partial(_conv_body, nt, kw, n_exp),
        grid=(bn, nt),
        in_specs=[pl.BlockSpec((1, HALO, d), lambda b, i: (b, jnp.maximum(i * hb - 1, 0), 0)),
                  pl.BlockSpec((1, tt, d), lambda b, i: (b, i, 0)),
                  pl.BlockSpec((1, HALO, d), lambda b, i: (b, jnp.minimum((i + 1) * hb, nhb - 1), 0)),
                  pl.BlockSpec((1, tt, d), lambda b, i: (b, i, 0)),
                  pl.BlockSpec((1, 6, d), lambda b, i: (mrow(b), 0, 0)),
                  pl.BlockSpec((kw, d), row),
                  pl.BlockSpec((1, d), row), pl.BlockSpec((1, d), row), pl.BlockSpec((1, d), row),
                  pl.BlockSpec((d, d), row), pl.BlockSpec((1, d), row), pl.BlockSpec((1, d), row),
                  pl.BlockSpec((d, 2 * LANES), row), pl.BlockSpec((d, LANES), row)],
        out_specs=out_specs,
        out_shape=out_shape,
        scratch_shapes=[pltpu.VMEM((tt + 2 * HALO, d), F32),
                        pltpu.VMEM((SUBLANES, tt + 2 * HALO - SUBLANES, LANES), F32),
                        pltpu.VMEM((tt, d), F32)],
        compiler_params=_cparams(("arbitrary", "arbitrary")),
        name="conv",
    )(u, u, u, x, m, dw, dwb, lng, lnb, w2, b2, gffn, rww, rwh)


def _proj_body(layer, x_ref, m_ref, g_ref, w_ref, lbl_ref, q_o, kf_o, kb_o, lf_o, lb_o, v_o, g_o):
    d = x_ref.shape[-1]
    m = m_ref[0]
    h = _prenorm(x_ref[0], g_ref[...], m[0:1], m[1:2]).astype(BF16)
    lg = lbl_ref[...]
    ex = jnp.exp(lg - jnp.max(lg, axis=0, keepdims=True))
    soft = ex / jnp.sum(ex, axis=0, keepdims=True)
    lbd = jnp.zeros(soft.shape[1:], F32)
    for l in range(1, layer + 1):
        lbd = lbd + soft[l]

    def col(j):
        return jnp.dot(h, w_ref[:, j * d:(j + 1) * d], preferred_element_type=F32)

    q_o[0] = col(0).astype(q_o.dtype)
    for dr, (k_o, l_o) in enumerate(((kf_o, lf_o), (kb_o, lb_o))):
        lb_row = lbd[dr:dr + 1]
        fg = lb_row + (1.0 - lb_row) * jax.nn.sigmoid(col(1 + dr))
        k_o[0] = (1.0 - fg).astype(k_o.dtype)
        l_o[0] = jnp.log(fg)
    v_o[0] = col(3).astype(v_o.dtype)
    g_o[0] = col(4).astype(g_o.dtype)


def _proj(x, m, mrow, g, w_in, lb_logits, layer, tt):
    bn, t, d = x.shape
    blk = pl.BlockSpec((1, tt, d), lambda b, i: (b, i, 0))
    sd = lambda dt: jax.ShapeDtypeStruct((bn, t, d), dt)
    return pl.pallas_call(
        functools.partial(_proj_body, layer),
        grid=(bn, t // tt),
        in_specs=[blk,
                  pl.BlockSpec((1, 6, d), lambda b, i: (mrow(b), 0, 0)),
                  pl.BlockSpec((1, d), lambda b, i: (0, 0)),
                  pl.BlockSpec((d, 5 * d), lambda b, i: (0, 0)),
                  pl.BlockSpec(lb_logits.shape, lambda b, i: (0, 0, 0))],
        out_specs=[blk] * 7,
        out_shape=[sd(BF16), sd(BF16), sd(BF16), sd(F32), sd(F32), sd(BF16), sd(BF16)],
        compiler_params=_cparams(("arbitrary", "arbitrary")),
        name="hgproj",
    )(x, m, g, w_in, lb_logits)


def _scan_chunk(reverse, q_ref, k_ref, v_ref, lf_ref, o_ref, st_ref, tmp_ref, rows):
    d = q_ref.shape[2]
    r = lax.broadcasted_iota(I32, (CHUNK, CHUNK), 0)
    c = lax.broadcasted_iota(I32, (CHUNK, CHUNK), 1)
    tri = ((c >= r) if reverse else (c <= r)).astype(BF16)
    rh = lax.broadcasted_iota(I32, (HALF, HALF), 0)
    ch = lax.broadcasted_iota(I32, (HALF, HALF), 1)
    keep_half = (ch >= rh) if reverse else (ch <= rh)
    tot_row = 0 if reverse else CHUNK - 1

    lf = lf_ref[0, rows, :]
    lf_hi, lf_lo = _split_bf16(lf)
    g = (jnp.dot(tri, lf_hi, preferred_element_type=F32)
         + jnp.dot(tri, lf_lo, preferred_element_type=F32))
    gtot = g[tot_row:tot_row + 1]
    q = q_ref[0, rows, :].astype(F32)
    k = k_ref[0, rows, :].astype(F32)
    v = v_ref[0, rows, :]
    q_in = (q * jnp.exp(g)).astype(BF16)
    k_st = (k * jnp.exp(gtot - g)).astype(BF16)
    d_st = jnp.exp(gtot)

    halves = (slice(0, HALF), slice(HALF, CHUNK))

    def max_abs(x):
        return jnp.max(jnp.max(jnp.abs(x), axis=0, keepdims=True), axis=1, keepdims=True)[0, 0]

    def centre_and_reach(rows_):
        lo_row, hi_row = g[rows_.start:rows_.start + 1], g[rows_.stop - 1:rows_.stop]
        return 0.5 * (lo_row + hi_row), max_abs(0.5 * (hi_row - lo_row))

    gmid, reach = centre_and_reach(slice(0, CHUNK))
    half_refs, half_reach = zip(*[centre_and_reach(hs) for hs in halves])
    whole_ok = reach <= EXP_RANGE
    halves_ok = jnp.maximum(half_reach[0], half_reach[1]) <= EXP_RANGE

    def inter_and_state(h, sl):
        st = st_ref[h]
        o = lax.dot_general(q_in[:, sl], st.astype(BF16), CONTRACT_LAST, preferred_element_type=F32)
        st_ref[h] = st * d_st[:, sl] + lax.dot_general(
            v[:, sl], k_st[:, sl], CONTRACT_FIRST, preferred_element_type=F32)
        return o

    @pl.when(whole_ok)
    def _():
        keep = (c >= r) if reverse else (c <= r)
        q_mid = (q * jnp.exp(g - gmid)).astype(BF16)
        k_mid = (k * jnp.exp(gmid - g)).astype(BF16)
        for h in range(d // HEAD):
            sl = slice(h * HEAD, (h + 1) * HEAD)
            sc = lax.dot_general(q_mid[:, sl], k_mid[:, sl], CONTRACT_LAST,
                                 preferred_element_type=F32)
            sc = jnp.where(keep, sc, 0.0).astype(BF16)
            o_ref[0, rows, sl] = (inter_and_state(h, sl)
                                  + jnp.dot(sc, v[:, sl], preferred_element_type=F32))

    @pl.when(jnp.logical_not(whole_ok) & halves_ok)
    def _():
        q_dg = [(q[hs] * jnp.exp(g[hs] - ref)).astype(BF16) for hs, ref in zip(halves, half_refs)]
        k_dg = [(k[hs] * jnp.exp(ref - g[hs])).astype(BF16) for hs, ref in zip(halves, half_refs)]
        qs, ks = (halves[0], halves[1]) if reverse else (halves[1], halves[0])
        edge = g[HALF:HALF + 1] if reverse else g[HALF - 1:HALF]
        q_x = (q[qs] * jnp.exp(g[qs] - edge)).astype(BF16)
        k_x = (k[ks] * jnp.exp(edge - g[ks])).astype(BF16)
        seen = 1 if reverse else 0
        for h in range(d // HEAD):
            sl = slice(h * HEAD, (h + 1) * HEAD)
            diag = [jnp.where(keep_half,
                              lax.dot_general(q_dg[i][:, sl], k_dg[i][:, sl], CONTRACT_LAST,
                                              preferred_element_type=F32), 0.0) for i in range(2)]
            cross = lax.dot_general(q_x[:, sl], k_x[:, sl], CONTRACT_LAST,
                                    preferred_element_type=F32)
            vh = [v[hs, sl] for hs in halves]
            intra = [jnp.dot(diag[i].astype(BF16), vh[i], preferred_element_type=F32)
                     for i in range(2)]
            intra[1 - seen] = intra[1 - seen] + jnp.dot(cross.astype(BF16), vh[seen],
                                                        preferred_element_type=F32)
            o = inter_and_state(h, sl)
            for i, hs in enumerate(halves):
                o_ref[0, slice(rows.start + hs.start, rows.start + hs.stop), sl] = o[hs] + intra[i]

    @pl.when(jnp.logical_not(whole_ok | halves_ok))
    def _():
        row_id = lax.broadcasted_iota(I32, (ROW_ALIGN, d), 0)
        ngroup = CHUNK // ROW_ALIGN

        def group(gi, carry):
            gidx = (ngroup - 1 - gi) if reverse else gi
            r0 = pl.multiple_of(rows.start + gidx * ROW_ALIGN, ROW_ALIGN)
            grp = pl.ds(r0, ROW_ALIGN)
            tmp_ref[0] = q_ref[0, grp, :].astype(F32)
            tmp_ref[1] = k_ref[0, grp, :].astype(F32)
            tmp_ref[2] = v_ref[0, grp, :].astype(F32)
            tmp_ref[3] = jnp.exp(lf_ref[0, grp, :])
            tmp_ref[4] = jnp.zeros((ROW_ALIGN, d), F32)

            def token(ti, carry2):
                t = (ROW_ALIGN - 1 - ti) if reverse else ti
                spread_row = lambda j: jnp.broadcast_to(tmp_ref[j, pl.ds(t, 1), :], (ROW_ALIGN, d))
                only_first = lambda a: jnp.where(row_id == 0, a, 0.0).astype(BF16)
                q_all = spread_row(0).astype(BF16)
                k_one, v_one = only_first(spread_row(1)), only_first(spread_row(2))
                f_row = tmp_ref[3, pl.ds(t, 1), :]
                pieces = []
                for h in range(d // HEAD):
                    sl = slice(h * HEAD, (h + 1) * HEAD)
                    st = st_ref[h] * f_row[:, sl] + lax.dot_general(
                        v_one[:, sl], k_one[:, sl], CONTRACT_FIRST, preferred_element_type=F32)
                    st_ref[h] = st
                    pieces.append(lax.dot_general(q_all[:, sl], st.astype(BF16), CONTRACT_LAST,
                                                  preferred_element_type=F32))
                tmp_ref[4] = jnp.where(row_id == t, jnp.concatenate(pieces, axis=1), tmp_ref[4])
                return carry2

            lax.fori_loop(0, ROW_ALIGN, token, 0)
            o_ref[0, grp, :] = tmp_ref[4]
            return carry

        lax.fori_loop(0, ngroup, group, 0)


def _scan_body(nsteps, qf_ref, kf_ref, vf_ref, lf_ref, qb_ref, kb_ref, vb_ref, lb_ref, s0f_ref,
               s0b_ref, of_ref, ob_ref, sff_ref, sfb_ref, stf_ref, stb_ref, tmp_ref):
    step = pl.program_id(1)
    nchunk = qf_ref.shape[1] // CHUNK

    @pl.when(step == 0)
    def _():
        stf_ref[...] = s0f_ref[0]
        stb_ref[...] = s0b_ref[0]

    for ci in range(nchunk):
        cf, cb = ci, nchunk - 1 - ci
        _scan_chunk(False, qf_ref, kf_ref, vf_ref, lf_ref, of_ref, stf_ref, tmp_ref,
                    slice(cf * CHUNK, (cf + 1) * CHUNK))
        _scan_chunk(True, qb_ref, kb_ref, vb_ref, lb_ref, ob_ref, stb_ref, tmp_ref,
                    slice(cb * CHUNK, (cb + 1) * CHUNK))

    @pl.when(step == nsteps - 1)
    def _():
        sff_ref[0] = stf_ref[...]
        sfb_ref[0] = stb_ref[...]


def _scan(q, kf, kb, v, lf, lb, s0f, s0b, tq):
    bn, t, d = q.shape
    nh = d // HEAD
    nsteps = t // tq
    fblk = pl.BlockSpec((1, tq, d), lambda b, i: (b, i, 0))
    bblk = pl.BlockSpec((1, tq, d), lambda b, i: (b, nsteps - 1 - i, 0))
    sblk = pl.BlockSpec((1, nh, HEAD, HEAD), lambda b, i: (b, 0, 0, 0))
    st_shape = jax.ShapeDtypeStruct((bn, nh, HEAD, HEAD), F32)
    o_shape = jax.ShapeDtypeStruct((bn, t, d), F32)
    return pl.pallas_call(
        functools.partial(_scan_body, nsteps),
        grid=(bn, nsteps),
        in_specs=[fblk] * 4 + [bblk] * 4 + [sblk, sblk],
        out_specs=[fblk, bblk, sblk, sblk],
        out_shape=[o_shape, o_shape, st_shape, st_shape],
        scratch_shapes=[pltpu.VMEM((nh, HEAD, HEAD), F32), pltpu.VMEM((nh, HEAD, HEAD), F32),
                        pltpu.VMEM((5, ROW_ALIGN, d), F32)],
        compiler_params=_cparams(("arbitrary", "arbitrary")),
        name="hgscan",
    )(q, kf, v, lf, q, kb, v, lb, s0f, s0b)


def _hgpost_body(n_exp, of_ref, ob_ref, g_ref, x_ref, m_ref, ng_ref, wo_ref, gffn_ref, rww_ref,
                 rwh_ref, lat_o, hx_o, aff_o):
    o = of_ref[0] + ob_ref[0]
    d = o.shape[-1]
    parts = []
    for h in range(d // HEAD):
        oh = o[:, h * HEAD:(h + 1) * HEAD]
        ms = jnp.mean(oh * oh, axis=-1, keepdims=True)
        parts.append(oh * lax.rsqrt(ms + EPS))
    on = jnp.concatenate(parts, axis=1) * ng_ref[...]
    y = (on * _silu(g_ref[0].astype(F32))).astype(BF16)
    z = jnp.dot(y, wo_ref[...], preferred_element_type=F32)
    _post(x_ref[0], z, m_ref[0], gffn_ref[...], rww_ref, rwh_ref, n_exp, lat_o, hx_o, aff_o)


def _hgpost(o_fw, o_bw, g, x, m, mrow, ng_tiled, w_out, gffn, rww, rwh, n_exp, tt):
    bn, t, d = x.shape
    blk = pl.BlockSpec((1, tt, d), lambda b, i: (b, i, 0))
    row = lambda b, i: (0, 0)
    out_specs, out_shape = _post_specs(bn, t, d, n_exp, tt)
    return pl.pallas_call(
        functools.partial(_hgpost_body, n_exp),
        grid=(bn, t // tt),
        in_specs=[blk, blk, blk, blk,
                  pl.BlockSpec((1, 6, d), lambda b, i: (mrow(b), 0, 0)),
                  pl.BlockSpec((1, d), row), pl.BlockSpec((d, d), row), pl.BlockSpec((1, d), row),
                  pl.BlockSpec((d, 2 * LANES), row), pl.BlockSpec((d, LANES), row)],
        out_specs=out_specs,
        out_shape=out_shape,
        compiler_params=_cparams(("arbitrary", "arbitrary")),
        name="hgpost",
    )(o_fw, o_bw, g, x, m, ng_tiled, w_out, gffn, rww, rwh)


def _route_body(n_exp, nb, k_sel, a_ref, code_e_o, code_t_o, lo_o, code_s):
    rows = n_exp * nb
    shift = nb.bit_length() - 1
    a3 = a_ref[0].reshape(n_exp, nb, LANES)

    def count(mask3):
        s = jnp.sum(jnp.where(mask3, 1.0, 0.0), axis=1, keepdims=True)
        return jnp.sum(s, axis=2, keepdims=True)

    thr = jnp.zeros((n_exp, 1, 1), I32)
    for bit in range(30, -1, -1):
        cand = thr | (1 << bit)
        enough = count(a3 >= lax.bitcast_convert_type(cand, F32)) >= k_sel
        thr = jnp.where(enough, cand, thr)
    lo = lax.bitcast_convert_type(thr, F32)
    hi = lax.bitcast_convert_type(thr + 1, F32)
    for _ in range(REFINE_STEPS):
        mid = 0.5 * (lo + hi)
        enough = count(a3 >= mid) >= k_sel
        lo = jnp.where(enough, mid, lo)
        hi = jnp.where(enough, hi, mid)
    gt3 = a3 >= hi
    eq3 = (a3 >= lo) & jnp.logical_not(gt3)
    need = k_sel - count(gt3)

    ri = lax.broadcasted_iota(I32, (rows, rows), 0)
    ci = lax.broadcasted_iota(I32, (rows, rows), 1)
    lblk = (((ri >> shift) == (ci >> shift)) & (ci < ri)).astype(BF16)
    ui = lax.broadcasted_iota(I32, (LANES, LANES), 0)
    uj = lax.broadcasted_iota(I32, (LANES, LANES), 1)
    upper = (ui <= uj).astype(BF16)
    ones = jnp.ones((LANES, LANES), BF16)

    def cumsum(x):
        xb = x.astype(BF16)
        within = jnp.dot(xb, upper, preferred_element_type=F32)
        part = jnp.dot(lblk, xb, preferred_element_type=F32)
        before = jnp.dot(part.astype(BF16), ones, preferred_element_type=F32)
        return within + before, before

    eqf = jnp.where(eq3, 1.0, 0.0).reshape(rows, LANES)
    eq_incl, _ = cumsum(eqf)
    eq_excl3 = (eq_incl - eqf).reshape(n_exp, nb, LANES)
    sel3 = gt3 | (eq3 & (eq_excl3 < need))
    self = jnp.where(sel3, 1.0, 0.0).reshape(rows, LANES)
    incl, before = cumsum(self)
    lo_o[0] = before.astype(I32)
    code = jnp.where(self > 0.0, incl - self, -1.0).astype(I32)
    code_e_o[0] = code
    code_s[...] = code

    fill = jnp.full((LANES - n_exp, LANES), -1, I32)
    for blk in range(nb):
        tile = code_s[pl.ds(blk, n_exp, stride=nb), :]
        code_t_o[0, blk * LANES:(blk + 1) * LANES, :] = jnp.concatenate([tile, fill], axis=0).T


def _route(aff, k_sel):
    bn, n_exp, t = aff.shape
    tp = max(t, SUBLANES * LANES)
    if tp != t:
        aff = jnp.pad(aff, ((0, 0), (0, 0), (0, tp - t)), constant_values=-1.0)
    nb = tp // LANES
    assert nb & (nb - 1) == 0
    rows = n_exp * nb
    blk = pl.BlockSpec((1, rows, LANES), lambda b: (b, 0, 0))
    code_e, code_t, lo = pl.pallas_call(
        functools.partial(_route_body, n_exp, nb, k_sel),
        grid=(bn,),
        in_specs=[blk],
        out_specs=[blk, pl.BlockSpec((1, tp, LANES), lambda b: (b, 0, 0)), blk],
        out_shape=[jax.ShapeDtypeStruct((bn, rows, LANES), I32),
                   jax.ShapeDtypeStruct((bn, tp, LANES), I32),
                   jax.ShapeDtypeStruct((bn, rows, LANES), I32)],
        scratch_shapes=[pltpu.VMEM((rows, LANES), I32)],
        compiler_params=_cparams(("arbitrary",)),
        name="route",
    )(aff.reshape(bn, rows, LANES))
    lo3 = lo[:, :, 0].reshape(bn, n_exp, nb)
    lo_ext = jnp.concatenate([lo3, jnp.full((bn, n_exp, 1), k_sel, I32)], axis=-1)
    return code_e.reshape(bn, n_exp, tp), code_t, lo_ext.reshape(-1), nb + 1


def _dispatch_body(n_exp, nbp, cap, lo_ref, hx_ref, code_ref, x_ref, xwin, carry, pend, sem):
    b = pl.program_id(0)
    i = pl.program_id(1)
    tt = hx_ref.shape[1]
    bpt = tt // LANES
    base = b * cap
    is_last = (b == pl.num_programs(0) - 1) & (i == pl.num_programs(1) - 1)

    @pl.when(i == 0)
    def _():
        carry[...] = jnp.zeros_like(carry)

    def out_copy(e, row):
        return pltpu.make_async_copy(xwin.at[e], x_ref.at[e, pl.ds(row, WIN)], sem.at[0])

    def drain():
        for e in range(n_exp):
            @pl.when(pend[e] == 1)
            def _(e=e):
                out_copy(e, 0).wait()
                pend[e] = 0

    @pl.when((b == 0) & (i == 0))
    def _():
        xwin[...] = jnp.zeros_like(xwin)
        for e in range(n_exp):
            out_copy(e, pl.num_programs(0) * cap).start()
            pend[e] = 1

    starts, next_blk = [], []
    npass = jnp.int32(0)
    for e in range(n_exp):
        o = (b * n_exp + e) * nbp + i * bpt
        a_e = (lo_ref[o] >> ROW_SHIFT) << ROW_SHIFT
        c_e = ((lo_ref[o + bpt] >> ROW_SHIFT) << ROW_SHIFT) - a_e
        starts.append(a_e)
        next_blk.append(c_e)
        npass = jnp.maximum(npass, (c_e >> WIN_SHIFT) + 1)

    h = hx_ref[0]
    code = code_ref[0]
    jrow = lax.broadcasted_iota(I32, (WIN, tt), 0)

    def one_pass(p, loop_carry):
        onehot = []
        for e in range(n_exp):
            rel = code[e:e + 1, :] - (starts[e] + p * WIN)
            onehot.append(jnp.where(rel == jrow, 1.0, 0.0).astype(BF16))
        xw = jnp.dot(jnp.concatenate(onehot, axis=0), h, preferred_element_type=F32)
        drain()
        for e in range(n_exp):
            xwin[e] = xw[e * WIN:(e + 1) * WIN].astype(BF16)

        @pl.when(p == 0)
        def _():
            for e in range(n_exp):
                xwin[e, 0:ROW_ALIGN, :] = xwin[e, 0:ROW_ALIGN, :] + carry[e]

        for e in range(n_exp):
            off = next_blk[e] - p * WIN

            @pl.when((off >= 0) & (off < WIN))
            def _(e=e, off=off):
                carry[e] = xwin[e, pl.ds(pl.multiple_of(off, ROW_ALIGN), ROW_ALIGN), :]

            @pl.when(off >= 0)
            def _(e=e):
                out_copy(e, pl.multiple_of(base + starts[e] + p * WIN, ROW_ALIGN)).start()
                pend[e] = 1
        return loop_carry

    lax.fori_loop(0, npass, one_pass, 0)

    @pl.when(is_last)
    def _():
        drain()


def _dispatch(hx, code_e, lo_ext, nbp, cap, tt):
    bn, t, dx = hx.shape
    n_exp = code_e.shape[1]
    assert cap % ROW_ALIGN == 0
    grid_spec = pltpu.PrefetchScalarGridSpec(
        num_scalar_prefetch=1,
        grid=(bn, t // tt),
        in_specs=[pl.BlockSpec((1, tt, dx), lambda b, i, lo: (b, i, 0)),
                  pl.BlockSpec((1, n_exp, tt), lambda b, i, lo: (b, 0, i))],
        out_specs=pl.BlockSpec(memory_space=pl.ANY),
        scratch_shapes=[pltpu.VMEM((n_exp, WIN, dx), BF16), pltpu.VMEM((n_exp, ROW_ALIGN, dx), BF16),
                        pltpu.SMEM((n_exp,), I32), pltpu.SemaphoreType.DMA((1,))],
    )
    return pl.pallas_call(
        functools.partial(_dispatch_body, n_exp, nbp, cap),
        grid_spec=grid_spec,
        out_shape=jax.ShapeDtypeStruct((n_exp, bn * cap + WIN, dx), BF16),
        compiler_params=pltpu.CompilerParams(dimension_semantics=("arbitrary", "arbitrary"),
                                             vmem_limit_bytes=VMEM_LIMIT_BYTES,
                                             has_side_effects=True),
        name="dispatch",
    )(lo_ext, hx, code_e)


def _ffn_body(d, nf, with_ctx, *refs):
    if with_ctx:
        x_ref, xc_ref, wg_ref, wu_ref, wd_ref, y_ref, yc_ref, acc_ref, accc_ref = refs
    else:
        x_ref, wg_ref, wu_ref, wd_ref, y_ref, acc_ref = refs
    e = pl.program_id(0)
    mi = pl.program_id(1)
    f = pl.program_id(2)
    wg = wg_ref[0, 0].astype(BF16)
    wu = wu_ref[0, 0].astype(BF16)
    wd = wd_ref[0, 0].astype(BF16)

    def run(xr, yr, acc):
        x = xr[0, :, :d]
        a = jnp.dot(x, wg, preferred_element_type=F32)
        u = jnp.dot(x, wu, preferred_element_type=F32)
        part = jnp.dot((_silu(a) * u).astype(BF16), wd, preferred_element_type=F32)

        @pl.when(f == 0)
        def _():
            acc[...] = part

        @pl.when(f > 0)
        def _():
            acc[...] += part

        @pl.when(f == nf - 1)
        def _():
            aff = xr[0, :, d:d + LANES].astype(F32) + xr[0, :, d + LANES:].astype(F32)
            lane = lax.broadcasted_iota(I32, aff.shape, 1)
            gate = jnp.sum(jnp.where(lane == e, aff, 0.0), axis=-1, keepdims=True)
            yr[0] = (acc[...] * gate).astype(yr.dtype)

    run(x_ref, y_ref, acc_ref)
    if with_ctx:
        @pl.when(mi == 0)
        def _():
            run(xc_ref, yc_ref, accc_ref)


def _ffn(xl, xc, w_gate, w_up, w_down, layer, tm, tmc, fc):
    n_exp, _, dx = xl.shape
    d = dx - 2 * LANES
    nm = (xl.shape[1] - WIN) // tm
    nf = w_gate.shape[-1] // fc
    with_ctx = xc is not None
    in_specs = [pl.BlockSpec((1, tm, dx), lambda e, mi, f: (e, mi, 0))]
    out_specs = [pl.BlockSpec((1, tm, d), lambda e, mi, f: (e, mi, 0))]
    out_shape = [jax.ShapeDtypeStruct((n_exp, nm * tm, d), BF16)]
    scratch = [pltpu.VMEM((tm, d), F32)]
    args = [xl]
    if with_ctx:
        in_specs.append(pl.BlockSpec((1, tmc, dx), lambda e, mi, f: (e, 0, 0)))
        out_specs.append(pl.BlockSpec((1, tmc, d), lambda e, mi, f: (e, 0, 0)))
        out_shape.append(jax.ShapeDtypeStruct((n_exp, tmc, d), BF16))
        scratch.append(pltpu.VMEM((tmc, d), F32))
        args.append(xc)
    in_specs += [pl.BlockSpec((1, 1, d, fc), lambda e, mi, f: (layer, e, 0, f)),
                 pl.BlockSpec((1, 1, d, fc), lambda e, mi, f: (layer, e, 0, f)),
                 pl.BlockSpec((1, 1, fc, d), lambda e, mi, f: (layer, e, f, 0))]
    outs = pl.pallas_call(
        functools.partial(_ffn_body, d, nf, with_ctx),
        grid=(n_exp, nm, nf),
        in_specs=in_specs,
        out_specs=out_specs,
        out_shape=out_shape,
        scratch_shapes=scratch,
        compiler_params=_cparams(("arbitrary", "arbitrary", "arbitrary")),
        name="ffn",
    )(*args, w_gate, w_up, w_down)
    return (outs[0], outs[1], nm * tm) if with_ctx else (outs[0], None, nm * tm)


def _combine_body(n_exp, nbp, cap, mtot, final, lo_ref, y_ref, x_ref, code_ref, m_ref,
                  gfin_ref, sel_ref, out_ref, ybuf, yextra, acc_ref, sem):
    b = pl.program_id(0)
    i = pl.program_id(1)
    nt = pl.num_programs(1)
    step = b * nt + i
    slot = step % 2
    tt, d = acc_ref.shape
    bpt = tt // LANES
    base = b * cap

    def tile_windows(bb, ii):
        firsts, npass = [], jnp.int32(0)
        for e in range(n_exp):
            o = (bb * n_exp + e) * nbp + ii * bpt
            a_e = bb * cap + ((lo_ref[o] >> ROW_SHIFT) << ROW_SHIFT)
            firsts.append(a_e)
            npass = jnp.maximum(npass, (bb * cap + lo_ref[o + bpt] - a_e + WIN - 1) >> WIN_SHIFT)
        return firsts, npass

    def clamp(w):
        return pl.multiple_of(jnp.minimum(w, mtot - WIN), ROW_ALIGN)

    def win_copy(e, row, buf, k):
        return pltpu.make_async_copy(y_ref.at[e, pl.ds(row, WIN)], buf.at[e], sem.at[k])

    starts, npass = tile_windows(b, i)

    @pl.when(step == 0)
    def _():
        for e in range(n_exp):
            win_copy(e, clamp(starts[e]), ybuf.at[0], 0).start()

    @pl.when(step + 1 < pl.num_programs(0) * nt)
    def _():
        wrap = i + 1 == nt
        nxt, _ = tile_windows(jnp.where(wrap, b + 1, b), jnp.where(wrap, 0, i + 1))
        for e in range(n_exp):
            win_copy(e, clamp(nxt[e]), ybuf.at[1 - slot], 1 - slot).start()

    acc_ref[...] = jnp.zeros_like(acc_ref)
    c1 = code_ref[0] + 1
    c_hi = (c1 >> CODE_SPLIT_SHIFT).astype(F32).astype(BF16)
    c_lo = (c1 & ((1 << CODE_SPLIT_SHIFT) - 1)).astype(F32).astype(BF16)
    spread = (jnp.dot(c_hi, sel_ref[...], preferred_element_type=F32) * float(1 << CODE_SPLIT_SHIFT)
              + jnp.dot(c_lo, sel_ref[...], preferred_element_type=F32))
    lane = lax.broadcasted_iota(I32, (1, n_exp * WIN), 1)
    lane_e = lane >> WIN_SHIFT
    lane_j = lane & (WIN - 1)

    def per_expert_row(vals):
        row = jnp.zeros_like(lane)
        for e in range(n_exp):
            row = jnp.where(lane_e == e, vals[e], row)
        return row

    def expand(buf, want, rows):
        slot_rank = (per_expert_row([r - base for r in rows]) + lane_j + 1).astype(F32)
        first_rank = (per_expert_row([w - base for w in want]) + 1).astype(F32)
        hit = (spread == slot_rank) & (spread >= first_rank)
        p_mat = jnp.where(hit, 1.0, 0.0).astype(BF16)
        acc_ref[...] += jnp.dot(p_mat, buf[...].reshape(n_exp * WIN, d), preferred_element_type=F32)

    rows0 = [clamp(a) for a in starts]
    for e in range(n_exp):
        win_copy(e, rows0[e], ybuf.at[slot], slot).wait()
    expand(ybuf.at[slot], starts, rows0)

    def later_pass(p, carry):
        want = [a + p * WIN for a in starts]
        rows = [clamp(w) for w in want]
        for e in range(n_exp):
            win_copy(e, rows[e], yextra, 2).start()
        for e in range(n_exp):
            win_copy(e, rows[e], yextra, 2).wait()
        expand(yextra, want, rows)
        return carry

    lax.fori_loop(1, npass, later_pass, 0)

    m = m_ref[0]
    lat_new = x_ref[0] + m[5:6] * acc_ref[...]
    if final:
        out_ref[0] = _rms(lat_new, gfin_ref[...])
    else:
        out_ref[0] = lat_new


def _combine(y, mtot, x, code_t, lo_ext, nbp, m, mrow, gfin, cap, final, tt):
    bn, t, d = x.shape
    n_exp = y.shape[0]
    assert cap >> CODE_SPLIT_SHIFT <= 256
    sel = (jnp.arange(LANES)[:, None] == jnp.arange(n_exp * WIN)[None, :] // WIN).astype(BF16)
    grid_spec = pltpu.PrefetchScalarGridSpec(
        num_scalar_prefetch=1,
        grid=(bn, t // tt),
        in_specs=[pl.BlockSpec(memory_space=pl.ANY),
                  pl.BlockSpec((1, tt, d), lambda b, i, lo: (b, i, 0)),
                  pl.BlockSpec((1, tt, LANES), lambda b, i, lo: (b, i, 0)),
                  pl.BlockSpec((1, 6, d), lambda b, i, lo: (mrow(b), 0, 0)),
                  pl.BlockSpec((1, d), lambda b, i, lo: (0, 0)),
                  pl.BlockSpec((LANES, n_exp * WIN), lambda b, i, lo: (0, 0))],
        out_specs=pl.BlockSpec((1, tt, d), lambda b, i, lo: (b, i, 0)),
        scratch_shapes=[pltpu.VMEM((2, n_exp, WIN, d), BF16), pltpu.VMEM((n_exp, WIN, d), BF16),
                        pltpu.VMEM((tt, d), F32), pltpu.SemaphoreType.DMA((3,))],
    )
    return pl.pallas_call(
        functools.partial(_combine_body, n_exp, nbp, cap, mtot, final),
        grid_spec=grid_spec,
        out_shape=jax.ShapeDtypeStruct((bn, t, d), F32),
        compiler_params=_cparams(("arbitrary", "arbitrary")),
        name="combine",
    )(lo_ext, y, x, code_t, m, gfin, sel)


def _moe(hx_l, aff_l, lat_mid, hx_c, aff_c, ctx_mid, m, layer, wg, wu, wd, final_g, last, tt_l, tt_c):
    bn, t, _ = hx_l.shape
    n_exp = aff_l.shape[1]
    with_ctx = hx_c is not None
    cap_l = CAPACITY_FACTOR * t // n_exp
    code_e_l, code_t_l, lo_l, nbp_l = _route(aff_l, cap_l)
    xl = _dispatch(hx_l, code_e_l, lo_l, nbp_l, cap_l, tt_l)
    xc, cap_c = None, 0
    if with_ctx:
        cap_c = CAPACITY_FACTOR * hx_c.shape[1] // n_exp
        code_e_c, code_t_c, lo_c, nbp_c = _route(aff_c, cap_c)
        xc = _dispatch(hx_c, code_e_c, lo_c, nbp_c, cap_c, tt_c)
    fc = min(1024, wg.shape[-1])
    yl, yc, rows_l = _ffn(xl, xc, wg, wu, wd, layer, cap_l, bn * cap_c, fc)
    lat_new = _combine(yl, rows_l, lat_mid, code_t_l, lo_l, nbp_l, m, lambda b: layer * 8 + b,
                       final_g, cap_l, last, tt_l)
    ctx_new = None
    if with_ctx:
        ctx_new = _combine(yc, bn * cap_c, ctx_mid, code_t_c, lo_c, nbp_c, m,
                           lambda b: layer * 8 + bn, final_g, cap_c, False, tt_c)
    return lat_new, ctx_new


def kernel(x, c, ctx, c_ctx, mod_w, mod_b, norm_mix_g, norm_ffn_g, cv_w1, cv_b1, cv_dw, cv_dwb,
           cv_ln_g, cv_ln_b, cv_w2, cv_b2, hg_w_in, hg_lb_logits, hg_norm_g, hg_w_out,
           router_w, exp_w_gate, exp_w_up, exp_w_down, final_g):
    bn, t, d = x.shape
    tc = ctx.shape[1]
    depth = mod_w.shape[0]
    n_exp = router_w.shape[-1]
    assert bn < 8 and d % HEAD == 0 and n_exp % 4 == 0 and n_exp <= LANES

    cond8 = jnp.concatenate([c, c_ctx[None], jnp.zeros((8 - bn - 1, d), F32)], axis=0)
    m = _mod(cond8, mod_w, mod_b).reshape(depth * 8, 6, d)

    tt_l = min(512, t)
    tt_c = min(256, tc)
    tmoe_l = min(256, t)
    tmoe_c = min(256, tc)
    row = lambda a: a.reshape(1, -1)
    fin = row(final_g)

    lat, cx = x, ctx
    for i in range(depth):
        last = i == depth - 1
        j = i // 2
        lat_row = lambda b, i=i: i * 8 + b
        ctx_row = lambda b, i=i: i * 8 + bn
        gmix, gffn = row(norm_mix_g[i]), row(norm_ffn_g[i])
        rw_pad = jnp.pad(router_w[i], ((0, 0), (0, LANES - n_exp)))
        rwh = rw_pad.astype(BF16)
        rww = jnp.concatenate([rwh, (rw_pad - rwh.astype(F32)).astype(BF16)], axis=1)
        need_ctx = not last
        hx_c = aff_c = ctx_mid = None
        if i % 2 == 0:
            w1 = cv_w1[j].astype(BF16)
            w2 = cv_w2[j].astype(BF16)
            cargs = (cv_dw[j], row(cv_dwb[j]), row(cv_ln_g[j]), row(cv_ln_b[j]), w2, row(cv_b2[j]),
                     gffn, rww, rwh, n_exp)
            u = _glu(lat, m, lat_row, gmix, w1, row(cv_b1[j]), tt_l)
            lat_mid, hx_l, aff_l = _conv(u, lat, m, lat_row, *cargs, tt_l)
            if need_ctx:
                uc = _glu(cx, m, ctx_row, gmix, w1, row(cv_b1[j]), tt_c)
                ctx_mid, hx_c, aff_c = _conv(uc, cx, m, ctx_row, *cargs, tt_c)
        else:
            w_in = hg_w_in[j].astype(BF16)
            w_out = hg_w_out[j].astype(BF16)
            ng = row(jnp.tile(hg_norm_g[j], d // HEAD))
            s_zero = jnp.zeros((bn, d // HEAD, HEAD, HEAD), F32)
            qc, kfc, kbc, lfc, lbc, vc, gc = _proj(cx, m, ctx_row, gmix, w_in, hg_lb_logits, i, tt_c)
            ql, kfl, kbl, lfl, lbl, vl, gl = _proj(lat, m, lat_row, gmix, w_in, hg_lb_logits, i, tt_l)
            tq_c, tq_l = min(256, tc), min(256, t)
            oc_f, oc_b, sc_f, sc_b = _scan(qc, kfc, kbc, vc, lfc, lbc, s_zero, s_zero, tq_c)
            ol_f, ol_b, _, _ = _scan(ql, kfl, kbl, vl, lfl, lbl, sc_f, sc_b, tq_l)
            pargs = (ng, w_out, gffn, rww, rwh, n_exp)
            lat_mid, hx_l, aff_l = _hgpost(ol_f, ol_b, gl, lat, m, lat_row, *pargs, tt_l)
            if need_ctx:
                ctx_mid, hx_c, aff_c = _hgpost(oc_f, oc_b, gc, cx, m, ctx_row, *pargs, tt_c)
        lat, cx_new = _moe(hx_l, aff_l, lat_mid, hx_c, aff_c, ctx_mid, m, i, exp_w_gate, exp_w_up,
                           exp_w_down, fin, last, tmoe_l, tmoe_c)
        if need_ctx:
            cx = cx_new
    return lat
```

```python
import functools

import jax
import jax.numpy as jnp
from jax import lax
from jax.experimental import pallas as pl
from jax.experimental.pallas import tpu as pltpu

F32 = jnp.float32
BF16 = jnp.bfloat16
I32 = jnp.int32
EPS = 1e-6
HIGHEST = lax.Precision.HIGHEST

LANES = 128
SUBLANES = 8
HEAD = 128
CHUNK = 128
HALF = CHUNK // 2
EXP_RANGE = 80.0
CAPACITY_FACTOR = 2
HALO = 16
WIN = 64
WIN_SHIFT = 6
ROW_ALIGN = 16
ROW_SHIFT = 4
REFINE_STEPS = 26
CODE_SPLIT_SHIFT = 5
VMEM_LIMIT_BYTES = 56 * 1024 * 1024

CONTRACT_LAST = (((1,), (1,)), ((), ()))
CONTRACT_FIRST = (((0,), (0,)), ((), ()))


def _cparams(sem):
    return pltpu.CompilerParams(dimension_semantics=sem, vmem_limit_bytes=VMEM_LIMIT_BYTES)


def _silu(x):
    return x * jax.nn.sigmoid(x)


def _rms(x, g):
    ms = jnp.mean(x * x, axis=-1, keepdims=True)
    return x * lax.rsqrt(ms + EPS) * g


def _prenorm(x, g, shift, scale):
    return _rms(x, g) * (1.0 + scale) + shift


def _split_bf16(x):
    hi = x.astype(BF16)
    return hi, (x - hi.astype(F32)).astype(BF16)


def _mod_body(c_ref, w_ref, b_ref, o_ref):
    c = c_ref[...]
    o_ref[0] = jnp.dot(_silu(c), w_ref[0], precision=HIGHEST,
                       preferred_element_type=F32) + b_ref[0]


def _mod(cond8, mod_w, mod_b):
    depth, d, d6 = mod_w.shape
    return pl.pallas_call(
        _mod_body,
        grid=(depth, d6 // d),
        in_specs=[pl.BlockSpec((8, d), lambda l, j: (0, 0)),
                  pl.BlockSpec((1, d, d), lambda l, j: (l, 0, j)),
                  pl.BlockSpec((1, 1, d), lambda l, j: (l, 0, j))],
        out_specs=pl.BlockSpec((1, 8, d), lambda l, j: (l, 0, j)),
        out_shape=jax.ShapeDtypeStruct((depth, 8, d6), F32),
        compiler_params=_cparams(("arbitrary", "arbitrary")),
        name="mod",
    )(cond8, mod_w, mod_b.reshape(depth, 1, d6))


def _post(lat, y, m, gffn, rww_ref, rwh_ref, n_exp, lat_o, hx_o, aff_o):
    d = lat.shape[-1]
    lat_new = lat + m[2:3] * y
    h = _prenorm(lat_new, gffn, m[3:4], m[4:5])
    h_hi, h_lo = _split_bf16(h)
    wide = jnp.dot(h_hi, rww_ref[...], preferred_element_type=F32)
    logits = (wide[:, :LANES] + wide[:, LANES:]
              + jnp.dot(h_lo, rwh_ref[...], preferred_element_type=F32))
    lane = lax.broadcasted_iota(I32, logits.shape, 1)
    lg = jnp.where(lane < n_exp, logits, -jnp.inf)
    ex = jnp.exp(lg - jnp.max(lg, axis=-1, keepdims=True))
    aff = ex / jnp.sum(ex, axis=-1, keepdims=True)
    a_hi, a_lo = _split_bf16(aff)
    lat_o[0] = lat_new
    hx_o[0, :, :d] = h_hi
    hx_o[0, :, d:d + LANES] = a_hi
    hx_o[0, :, d + LANES:] = a_lo
    aff_o[0] = aff.T[:n_exp]


def _post_specs(bn, t, d, n_exp, tt):
    specs = [pl.BlockSpec((1, tt, d), lambda b, i: (b, i, 0)),
             pl.BlockSpec((1, tt, d + 2 * LANES), lambda b, i: (b, i, 0)),
             pl.BlockSpec((1, n_exp, tt), lambda b, i: (b, 0, i))]
    shapes = [jax.ShapeDtypeStruct((bn, t, d), F32),
              jax.ShapeDtypeStruct((bn, t, d + 2 * LANES), BF16),
              jax.ShapeDtypeStruct((bn, n_exp, t), F32)]
    return specs, shapes


def _glu_body(x_ref, m_ref, g_ref, w1_ref, b1_ref, u_ref):
    d = x_ref.shape[-1]
    m = m_ref[0]
    h = _prenorm(x_ref[0], g_ref[...], m[0:1], m[1:2]).astype(BF16)
    a = jnp.dot(h, w1_ref[:, :d], preferred_element_type=F32) + b1_ref[:, :d]
    gate = jnp.dot(h, w1_ref[:, d:], preferred_element_type=F32) + b1_ref[:, d:]
    u_ref[0] = (a * jax.nn.sigmoid(gate)).astype(u_ref.dtype)


def _glu(x, m, mrow, g, w1, b1, tt):
    bn, t, d = x.shape
    return pl.pallas_call(
        _glu_body,
        grid=(bn, t // tt),
        in_specs=[pl.BlockSpec((1, tt, d), lambda b, i: (b, i, 0)),
                  pl.BlockSpec((1, 6, d), lambda b, i: (mrow(b), 0, 0)),
                  pl.BlockSpec((1, d), lambda b, i: (0, 0)),
                  pl.BlockSpec((d, 2 * d), lambda b, i: (0, 0)),
                  pl.BlockSpec((1, 2 * d), lambda b, i: (0, 0))],
        out_specs=pl.BlockSpec((1, tt, d), lambda b, i: (b, i, 0)),
        out_shape=jax.ShapeDtypeStruct((bn, t, d), BF16),
        compiler_params=_cparams(("arbitrary", "arbitrary")),
        name="glu",
    )(x, m, g, w1, b1)


def _conv_body(nt, kw, n_exp, up_ref, uc_ref, un_ref, x_ref, m_ref, dw_ref, dwb_ref, lng_ref,
               lnb_ref, w2_ref, b2_ref, gffn_ref, rww_ref, rwh_ref, lat_o, hx_o, aff_o,
               xs_ref, sh_ref, cv_ref):
    i = pl.program_id(1)
    tt, d = cv_ref.shape
    pad = (kw - 1) // 2
    xs_ref[0:HALO, :] = jnp.where(i > 0, up_ref[0].astype(F32), 0.0)
    xs_ref[HALO:HALO + tt, :] = uc_ref[0].astype(F32)
    xs_ref[HALO + tt:, :] = jnp.where(i < nt - 1, un_ref[0].astype(F32), 0.0)

    strip = min(tt, 128)
    span = sh_ref.shape[1]

    def chan_block(j, carry):
        c0 = pl.multiple_of(j * LANES, LANES)
        for s in range(SUBLANES):
            sh_ref[s] = xs_ref[s:s + span, pl.ds(c0, LANES)]
        for st in range(tt // strip):
            acc = jnp.zeros((strip, LANES), F32)
            for k in range(kw):
                off = k + HALO - pad
                r0 = st * strip + (off // SUBLANES) * SUBLANES
                acc = acc + sh_ref[off % SUBLANES, r0:r0 + strip, :] * dw_ref[k:k + 1, pl.ds(c0, LANES)]
            cv_ref[st * strip:(st + 1) * strip, pl.ds(c0, LANES)] = acc + dwb_ref[:, pl.ds(c0, LANES)]
        return carry

    lax.fori_loop(0, d // LANES, chan_block, 0)

    cv = cv_ref[...]
    mu = jnp.mean(cv, axis=-1, keepdims=True)
    xc = cv - mu
    var = jnp.mean(xc * xc, axis=-1, keepdims=True)
    y = xc * lax.rsqrt(var + EPS) * lng_ref[...] + lnb_ref[...]
    z = jnp.dot(_silu(y).astype(BF16), w2_ref[...], preferred_element_type=F32) + b2_ref[...]
    _post(x_ref[0], z, m_ref[0], gffn_ref[...], rww_ref, rwh_ref, n_exp, lat_o, hx_o, aff_o)


def _conv(u, x, m, mrow, dw, dwb, lng, lnb, w2, b2, gffn, rww, rwh, n_exp, tt):
    bn, t, d = x.shape
    nt = t // tt
    kw = dw.shape[0]
    assert (kw - 1) // 2 <= HALO
    hb = tt // HALO
    nhb = t // HALO
    row = lambda b, i: (0, 0)
    out_specs, out_shape = _post_specs(bn, t, d, n_exp, tt)
    return pl.pallas_call(
        functools.partial(_conv_body, nt, kw, n_exp),
        grid=(bn, nt),
        in_specs=[pl.BlockSpec((1, HALO, d), lambda b, i: (b, jnp.maximum(i * hb - 1, 0), 0)),
                  pl.BlockSpec((1, tt, d), lambda b, i: (b, i, 0)),
                  pl.BlockSpec((1, HALO, d), lambda b, i: (b, jnp.minimum((i + 1) * hb, nhb - 1), 0)),
                  pl.BlockSpec((1, tt, d), lambda b, i: (b, i, 0)),
                  pl.BlockSpec((1, 6, d), lambda b, i: (mrow(b), 0, 0)),
                  pl.BlockSpec((kw, d), row),
                  pl.BlockSpec((1, d), row), pl.BlockSpec((1, d), row), pl.BlockSpec((1, d), row),
                  pl.BlockSpec((d, d), row), pl.BlockSpec((1, d), row), pl.BlockSpec((1, d), row),
                  pl.BlockSpec((d, 2 * LANES), row), pl.BlockSpec((d, LANES), row)],
        out_specs=out_specs,
        out_shape=out_shape,
        scratch_shapes=[pltpu.VMEM((tt + 2 * HALO, d), F32),
                        pltpu.VMEM((SUBLANES, tt + 2 * HALO - SUBLANES, LANES), F32),
                        pltpu.VMEM((tt, d), F32)],
        compiler_params=_cparams(("arbitrary", "arbitrary")),
        name="conv",
    )(u, u, u, x, m, dw, dwb, lng, lnb, w2, b2, gffn, rww, rwh)


def _proj_body(layer, x_ref, m_ref, g_ref, w_ref, lbl_ref, q_o, kf_o, kb_o, lf_o, lb_o, v_o, g_o):
    d = x_ref.shape[-1]
    m = m_ref[0]
    h = _prenorm(x_ref[0], g_ref[...], m[0:1], m[1:2]).astype(BF16)
    lg = lbl_ref[...]
    ex = jnp.exp(lg - jnp.max(lg, axis=0, keepdims=True))
    soft = ex / jnp.sum(ex, axis=0, keepdims=True)
    lbd = jnp.zeros(soft.shape[1:], F32)
    for l in range(1, layer + 1):
        lbd = lbd + soft[l]

    def col(j):
        return jnp.dot(h, w_ref[:, j * d:(j + 1) * d], preferred_element_type=F32)

    q_o[0] = col(0).astype(q_o.dtype)
    for dr, (k_o, l_o) in enumerate(((kf_o, lf_o), (kb_o, lb_o))):
        lb_row = lbd[dr:dr + 1]
        fg = lb_row + (1.0 - lb_row) * jax.nn.sigmoid(col(1 + dr))
        k_o[0] = (1.0 - fg).astype(k_o.dtype)
        l_o[0] = jnp.log(fg)
    v_o[0] = col(3).astype(v_o.dtype)
    g_o[0] = col(4).astype(g_o.dtype)


def _proj(x, m, mrow, g, w_in, lb_logits, layer, tt):
    bn, t, d = x.shape
    blk = pl.BlockSpec((1, tt, d), lambda b, i: (b, i, 0))
    sd = lambda dt: jax.ShapeDtypeStruct((bn, t, d), dt)
    return pl.pallas_call(
        functools.partial(_proj_body, layer),
        grid=(bn, t // tt),
        in_specs=[blk,
                  pl.BlockSpec((1, 6, d), lambda b, i: (mrow(b), 0, 0)),
                  pl.BlockSpec((1, d), lambda b, i: (0, 0)),
                  pl.BlockSpec((d, 5 * d), lambda b, i: (0, 0)),
                  pl.BlockSpec(lb_logits.shape, lambda b, i: (0, 0, 0))],
        out_specs=[blk] * 7,
        out_shape=[sd(BF16), sd(BF16), sd(BF16), sd(F32), sd(F32), sd(BF16), sd(BF16)],
        compiler_params=_cparams(("arbitrary", "arbitrary")),
        name="hgproj",
    )(x, m, g, w_in, lb_logits)


def _scan_chunk(reverse, q_ref, k_ref, v_ref, lf_ref, o_ref, st_ref, tmp_ref, rows):
    d = q_ref.shape[2]
    r = lax.broadcasted_iota(I32, (CHUNK, CHUNK), 0)
    c = lax.broadcasted_iota(I32, (CHUNK, CHUNK), 1)
    tri = ((c >= r) if reverse else (c <= r)).astype(BF16)
    rh = lax.broadcasted_iota(I32, (HALF, HALF), 0)
    ch = lax.broadcasted_iota(I32, (HALF, HALF), 1)
    keep_half = (ch >= rh) if reverse else (ch <= rh)
    tot_row = 0 if reverse else CHUNK - 1

    lf = lf_ref[0, rows, :]
    lf_hi, lf_lo = _split_bf16(lf)
    g = (jnp.dot(tri, lf_hi, preferred_element_type=F32)
         + jnp.dot(tri, lf_lo, preferred_element_type=F32))
    gtot = g[tot_row:tot_row + 1]
    q = q_ref[0, rows, :].astype(F32)
    k = k_ref[0, rows, :].astype(F32)
    v = v_ref[0, rows, :]
    q_in = (q * jnp.exp(g)).astype(BF16)
    k_st = (k * jnp.exp(gtot - g)).astype(BF16)
    d_st = jnp.exp(gtot)

    halves = (slice(0, HALF), slice(HALF, CHUNK))

    def max_abs(x):
        return jnp.max(jnp.max(jnp.abs(x), axis=0, keepdims=True), axis=1, keepdims=True)[0, 0]

    def centre_and_reach(rows_):
        lo_row, hi_row = g[rows_.start:rows_.start + 1], g[rows_.stop - 1:rows_.stop]
        return 0.5 * (lo_row + hi_row), max_abs(0.5 * (hi_row - lo_row))

    gmid, reach = centre_and_reach(slice(0, CHUNK))
    half_refs, half_reach = zip(*[centre_and_reach(hs) for hs in halves])
    whole_ok = reach <= EXP_RANGE
    halves_ok = jnp.maximum(half_reach[0], half_reach[1]) <= EXP_RANGE

    def inter_and_state(h, sl):
        st = st_ref[h]
        o = lax.dot_general(q_in[:, sl], st.astype(BF16), CONTRACT_LAST, preferred_element_type=F32)
        st_ref[h] = st * d_st[:, sl] + lax.dot_general(
            v[:, sl], k_st[:, sl], CONTRACT_FIRST, preferred_element_type=F32)
        return o

    @pl.when(whole_ok)
    def _():
        keep = (c >= r) if reverse else (c <= r)
        q_mid = (q * jnp.exp(g - gmid)).astype(BF16)
        k_mid = (k * jnp.exp(gmid - g)).astype(BF16)
        for h in range(d // HEAD):
            sl = slice(h * HEAD, (h + 1) * HEAD)
            sc = lax.dot_general(q_mid[:, sl], k_mid[:, sl], CONTRACT_LAST,
                                 preferred_element_type=F32)
            sc = jnp.where(keep, sc, 0.0).astype(BF16)
            o_ref[0, rows, sl] = (inter_and_state(h, sl)
                                  + jnp.dot(sc, v[:, sl], preferred_element_type=F32)
                                  ).astype(o_ref.dtype)

    @pl.when(jnp.logical_not(whole_ok) & halves_ok)
    def _():
        q_dg = [(q[hs] * jnp.exp(g[hs] - ref)).astype(BF16) for hs, ref in zip(halves, half_refs)]
        k_dg = [(k[hs] * jnp.exp(ref - g[hs])).astype(BF16) for hs, ref in zip(halves, half_refs)]
        qs, ks = (halves[0], halves[1]) if reverse else (halves[1], halves[0])
        edge = g[HALF:HALF + 1] if reverse else g[HALF - 1:HALF]
        q_x = (q[qs] * jnp.exp(g[qs] - edge)).astype(BF16)
        k_x = (k[ks] * jnp.exp(edge - g[ks])).astype(BF16)
        seen = 1 if reverse else 0
        for h in range(d // HEAD):
            sl = slice(h * HEAD, (h + 1) * HEAD)
            diag = [jnp.where(keep_half,
                              lax.dot_general(q_dg[i][:, sl], k_dg[i][:, sl], CONTRACT_LAST,
                                              preferred_element_type=F32), 0.0) for i in range(2)]
            cross = lax.dot_general(q_x[:, sl], k_x[:, sl], CONTRACT_LAST,
                                    preferred_element_type=F32)
            vh = [v[hs, sl] for hs in halves]
            intra = [jnp.dot(diag[i].astype(BF16), vh[i], preferred_element_type=F32)
                     for i in range(2)]
            intra[1 - seen] = intra[1 - seen] + jnp.dot(cross.astype(BF16), vh[seen],
                                                        preferred_element_type=F32)
            o = inter_and_state(h, sl)
            for i, hs in enumerate(halves):
                o_ref[0, slice(rows.start + hs.start, rows.start + hs.stop), sl] = (
                    o[hs] + intra[i]).astype(o_ref.dtype)

    @pl.when(jnp.logical_not(whole_ok | halves_ok))
    def _():
        row_id = lax.broadcasted_iota(I32, (ROW_ALIGN, d), 0)
        ngroup = CHUNK // ROW_ALIGN

        def group(gi, carry):
            gidx = (ngroup - 1 - gi) if reverse else gi
            r0 = pl.multiple_of(rows.start + gidx * ROW_ALIGN, ROW_ALIGN)
            grp = pl.ds(r0, ROW_ALIGN)
            tmp_ref[0] = q_ref[0, grp, :].astype(F32)
            tmp_ref[1] = k_ref[0, grp, :].astype(F32)
            tmp_ref[2] = v_ref[0, grp, :].astype(F32)
            tmp_ref[3] = jnp.exp(lf_ref[0, grp, :])
            tmp_ref[4] = jnp.zeros((ROW_ALIGN, d), F32)

            def token(ti, carry2):
                t = (ROW_ALIGN - 1 - ti) if reverse else ti
                spread_row = lambda j: jnp.broadcast_to(tmp_ref[j, pl.ds(t, 1), :], (ROW_ALIGN, d))
                only_first = lambda a: jnp.where(row_id == 0, a, 0.0).astype(BF16)
                q_all = spread_row(0).astype(BF16)
                k_one, v_one = only_first(spread_row(1)), only_first(spread_row(2))
                f_row = tmp_ref[3, pl.ds(t, 1), :]
                pieces = []
                for h in range(d // HEAD):
                    sl = slice(h * HEAD, (h + 1) * HEAD)
                    st = st_ref[h] * f_row[:, sl] + lax.dot_general(
                        v_one[:, sl], k_one[:, sl], CONTRACT_FIRST, preferred_element_type=F32)
                    st_ref[h] = st
                    pieces.append(lax.dot_general(q_all[:, sl], st.astype(BF16), CONTRACT_LAST,
                                                  preferred_element_type=F32))
                tmp_ref[4] = jnp.where(row_id == t, jnp.concatenate(pieces, axis=1), tmp_ref[4])
                return carry2

            lax.fori_loop(0, ROW_ALIGN, token, 0)
            o_ref[0, grp, :] = tmp_ref[4].astype(o_ref.dtype)
            return carry

        lax.fori_loop(0, ngroup, group, 0)


def _scan_body(nsteps, qf_ref, kf_ref, vf_ref, lf_ref, qb_ref, kb_ref, vb_ref, lb_ref, s0f_ref,
               s0b_ref, of_ref, ob_ref, sff_ref, sfb_ref, stf_ref, stb_ref, tmp_ref):
    step = pl.program_id(1)
    nchunk = qf_ref.shape[1] // CHUNK

    @pl.when(step == 0)
    def _():
        stf_ref[...] = s0f_ref[0]
        stb_ref[...] = s0b_ref[0]

    for ci in range(nchunk):
        cf, cb = ci, nchunk - 1 - ci
        _scan_chunk(False, qf_ref, kf_ref, vf_ref, lf_ref, of_ref, stf_ref, tmp_ref,
                    slice(cf * CHUNK, (cf + 1) * CHUNK))
        _scan_chunk(True, qb_ref, kb_ref, vb_ref, lb_ref, ob_ref, stb_ref, tmp_ref,
                    slice(cb * CHUNK, (cb + 1) * CHUNK))

    @pl.when(step == nsteps - 1)
    def _():
        sff_ref[0] = stf_ref[...]
        sfb_ref[0] = stb_ref[...]


def _scan(q, kf, kb, v, lf, lb, s0f, s0b, tq):
    bn, t, d = q.shape
    nh = d // HEAD
    nsteps = t // tq
    fblk = pl.BlockSpec((1, tq, d), lambda b, i: (b, i, 0))
    bblk = pl.BlockSpec((1, tq, d), lambda b, i: (b, nsteps - 1 - i, 0))
    sblk = pl.BlockSpec((1, nh, HEAD, HEAD), lambda b, i: (b, 0, 0, 0))
    st_shape = jax.ShapeDtypeStruct((bn, nh, HEAD, HEAD), F32)
    o_shape = jax.ShapeDtypeStruct((bn, t, d), BF16)
    return pl.pallas_call(
        functools.partial(_scan_body, nsteps),
        grid=(bn, nsteps),
        in_specs=[fblk] * 4 + [bblk] * 4 + [sblk, sblk],
        out_specs=[fblk, bblk, sblk, sblk],
        out_shape=[o_shape, o_shape, st_shape, st_shape],
        scratch_shapes=[pltpu.VMEM((nh, HEAD, HEAD), F32), pltpu.VMEM((nh, HEAD, HEAD), F32),
                        pltpu.VMEM((5, ROW_ALIGN, d), F32)],
        compiler_params=_cparams(("arbitrary", "arbitrary")),
        name="hgscan",
    )(q, kf, v, lf, q, kb, v, lb, s0f, s0b)


def _hgpost_body(n_exp, of_ref, ob_ref, g_ref, x_ref, m_ref, ng_ref, wo_ref, gffn_ref, rww_ref,
                 rwh_ref, lat_o, hx_o, aff_o):
    o = of_ref[0].astype(F32) + ob_ref[0].astype(F32)
    d = o.shape[-1]
    parts = []
    for h in range(d // HEAD):
        oh = o[:, h * HEAD:(h + 1) * HEAD]
        ms = jnp.mean(oh * oh, axis=-1, keepdims=True)
        parts.append(oh * lax.rsqrt(ms + EPS))
    on = jnp.concatenate(parts, axis=1) * ng_ref[...]
    y = (on * _silu(g_ref[0].astype(F32))).astype(BF16)
    z = jnp.dot(y, wo_ref[...], preferred_element_type=F32)
    _post(x_ref[0], z, m_ref[0], gffn_ref[...], rww_ref, rwh_ref, n_exp, lat_o, hx_o, aff_o)


def _hgpost(o_fw, o_bw, g, x, m, mrow, ng_tiled, w_out, gffn, rww, rwh, n_exp, tt):
    bn, t, d = x.shape
    blk = pl.BlockSpec((1, tt, d), lambda b, i: (b, i, 0))
    row = lambda b, i: (0, 0)
    out_specs, out_shape = _post_specs(bn, t, d, n_exp, tt)
    return pl.pallas_call(
        functools.partial(_hgpost_body, n_exp),
        grid=(bn, t // tt),
        in_specs=[blk, blk, blk, blk,
                  pl.BlockSpec((1, 6, d), lambda b, i: (mrow(b), 0, 0)),
                  pl.BlockSpec((1, d), row), pl.BlockSpec((d, d), row), pl.BlockSpec((1, d), row),
                  pl.BlockSpec((d, 2 * LANES), row), pl.BlockSpec((d, LANES), row)],
        out_specs=out_specs,
        out_shape=out_shape,
        compiler_params=_cparams(("arbitrary", "arbitrary")),
        name="hgpost",
    )(o_fw, o_bw, g, x, m, ng_tiled, w_out, gffn, rww, rwh)


def _route_body(n_exp, nb, k_sel, a_ref, code_e_o, code_t_o, lo_o, code_s):
    rows = n_exp * nb
    shift = nb.bit_length() - 1
    a3 = a_ref[0].reshape(n_exp, nb, LANES)

    def count(mask3):
        s = jnp.sum(jnp.where(mask3, 1.0, 0.0), axis=1, keepdims=True)
        return jnp.sum(s, axis=2, keepdims=True)

    thr = jnp.zeros((n_exp, 1, 1), I32)
    for bit in range(30, -1, -1):
        cand = thr | (1 << bit)
        enough = count(a3 >= lax.bitcast_convert_type(cand, F32)) >= k_sel
        thr = jnp.where(enough, cand, thr)
    lo = lax.bitcast_convert_type(thr, F32)
    hi = lax.bitcast_convert_type(thr + 1, F32)
    for _ in range(REFINE_STEPS):
        mid = 0.5 * (lo + hi)
        enough = count(a3 >= mid) >= k_sel
        lo = jnp.where(enough, mid, lo)
        hi = jnp.where(enough, hi, mid)
    gt3 = a3 >= hi
    eq3 = (a3 >= lo) & jnp.logical_not(gt3)
    need = k_sel - count(gt3)

    ri = lax.broadcasted_iota(I32, (rows, rows), 0)
    ci = lax.broadcasted_iota(I32, (rows, rows), 1)
    lblk = (((ri >> shift) == (ci >> shift)) & (ci < ri)).astype(BF16)
    ui = lax.broadcasted_iota(I32, (LANES, LANES), 0)
    uj = lax.broadcasted_iota(I32, (LANES, LANES), 1)
    upper = (ui <= uj).astype(BF16)
    ones = jnp.ones((LANES, LANES), BF16)

    def cumsum(x):
        xb = x.astype(BF16)
        within = jnp.dot(xb, upper, preferred_element_type=F32)
        part = jnp.dot(lblk, xb, preferred_element_type=F32)
        before = jnp.dot(part.astype(BF16), ones, preferred_element_type=F32)
        return within + before, before

    eqf = jnp.where(eq3, 1.0, 0.0).reshape(rows, LANES)
    eq_incl, _ = cumsum(eqf)
    eq_excl3 = (eq_incl - eqf).reshape(n_exp, nb, LANES)
    sel3 = gt3 | (eq3 & (eq_excl3 < need))
    self = jnp.where(sel3, 1.0, 0.0).reshape(rows, LANES)
    incl, before = cumsum(self)
    lo_o[0] = before.astype(I32)
    code = jnp.where(self > 0.0, incl - self, -1.0).astype(I32)
    code_e_o[0] = code
    code_s[...] = code

    fill = jnp.full((LANES - n_exp, LANES), -1, I32)
    for blk in range(nb):
        tile = code_s[pl.ds(blk, n_exp, stride=nb), :]
        code_t_o[0, blk * LANES:(blk + 1) * LANES, :] = jnp.concatenate([tile, fill], axis=0).T


def _route(aff, k_sel):
    bn, n_exp, t = aff.shape
    tp = max(t, SUBLANES * LANES)
    if tp != t:
        aff = jnp.pad(aff, ((0, 0), (0, 0), (0, tp - t)), constant_values=-1.0)
    nb = tp // LANES
    assert nb & (nb - 1) == 0
    rows = n_exp * nb
    blk = pl.BlockSpec((1, rows, LANES), lambda b: (b, 0, 0))
    code_e, code_t, lo = pl.pallas_call(
        functools.partial(_route_body, n_exp, nb, k_sel),
        grid=(bn,),
        in_specs=[blk],
        out_specs=[blk, pl.BlockSpec((1, tp, LANES), lambda b: (b, 0, 0)), blk],
        out_shape=[jax.ShapeDtypeStruct((bn, rows, LANES), I32),
                   jax.ShapeDtypeStruct((bn, tp, LANES), I32),
                   jax.ShapeDtypeStruct((bn, rows, LANES), I32)],
        scratch_shapes=[pltpu.VMEM((rows, LANES), I32)],
        compiler_params=_cparams(("arbitrary",)),
        name="route",
    )(aff.reshape(bn, rows, LANES))
    lo3 = lo[:, :, 0].reshape(bn, n_exp, nb)
    lo_ext = jnp.concatenate([lo3, jnp.full((bn, n_exp, 1), k_sel, I32)], axis=-1)
    return code_e.reshape(bn, n_exp, tp), code_t, lo_ext.reshape(-1), nb + 1


def _dispatch_body(n_exp, nbp, cap, lo_ref, hx_ref, code_ref, x_ref, xwin, carry, pend, sem):
    b = pl.program_id(0)
    i = pl.program_id(1)
    tt = hx_ref.shape[1]
    bpt = tt // LANES
    base = b * cap
    is_last = (b == pl.num_programs(0) - 1) & (i == pl.num_programs(1) - 1)

    @pl.when(i == 0)
    def _():
        carry[...] = jnp.zeros_like(carry)

    def out_copy(e, row):
        return pltpu.make_async_copy(xwin.at[e], x_ref.at[e, pl.ds(row, WIN)], sem.at[0])

    def drain():
        for e in range(n_exp):
            @pl.when(pend[e] == 1)
            def _(e=e):
                out_copy(e, 0).wait()
                pend[e] = 0

    @pl.when((b == 0) & (i == 0))
    def _():
        xwin[...] = jnp.zeros_like(xwin)
        for e in range(n_exp):
            out_copy(e, pl.num_programs(0) * cap).start()
            pend[e] = 1

    starts, next_blk = [], []
    npass = jnp.int32(0)
    for e in range(n_exp):
        o = (b * n_exp + e) * nbp + i * bpt
        a_e = (lo_ref[o] >> ROW_SHIFT) << ROW_SHIFT
        c_e = ((lo_ref[o + bpt] >> ROW_SHIFT) << ROW_SHIFT) - a_e
        starts.append(a_e)
        next_blk.append(c_e)
        npass = jnp.maximum(npass, (c_e >> WIN_SHIFT) + 1)

    h = hx_ref[0]
    code = code_ref[0]
    jrow = lax.broadcasted_iota(I32, (WIN, tt), 0)

    def one_pass(p, loop_carry):
        onehot = []
        for e in range(n_exp):
            rel = code[e:e + 1, :] - (starts[e] + p * WIN)
            onehot.append(jnp.where(rel == jrow, 1.0, 0.0).astype(BF16))
        xw = jnp.dot(jnp.concatenate(onehot, axis=0), h, preferred_element_type=F32)
        drain()
        for e in range(n_exp):
            xwin[e] = xw[e * WIN:(e + 1) * WIN].astype(BF16)

        @pl.when(p == 0)
        def _():
            for e in range(n_exp):
                xwin[e, 0:ROW_ALIGN, :] = xwin[e, 0:ROW_ALIGN, :] + carry[e]

        for e in range(n_exp):
            off = next_blk[e] - p * WIN

            @pl.when((off >= 0) & (off < WIN))
            def _(e=e, off=off):
                carry[e] = xwin[e, pl.ds(pl.multiple_of(off, ROW_ALIGN), ROW_ALIGN), :]

            @pl.when(off >= 0)
            def _(e=e):
                out_copy(e, pl.multiple_of(base + starts[e] + p * WIN, ROW_ALIGN)).start()
                pend[e] = 1
        return loop_carry

    lax.fori_loop(0, npass, one_pass, 0)

    @pl.when(is_last)
    def _():
        drain()


def _dispatch(hx, code_e, lo_ext, nbp, cap, tt):
    bn, t, dx = hx.shape
    n_exp = code_e.shape[1]
    assert cap % ROW_ALIGN == 0
    grid_spec = pltpu.PrefetchScalarGridSpec(
        num_scalar_prefetch=1,
        grid=(bn, t // tt),
        in_specs=[pl.BlockSpec((1, tt, dx), lambda b, i, lo: (b, i, 0)),
                  pl.BlockSpec((1, n_exp, tt), lambda b, i, lo: (b, 0, i))],
        out_specs=pl.BlockSpec(memory_space=pl.ANY),
        scratch_shapes=[pltpu.VMEM((n_exp, WIN, dx), BF16), pltpu.VMEM((n_exp, ROW_ALIGN, dx), BF16),
                        pltpu.SMEM((n_exp,), I32), pltpu.SemaphoreType.DMA((1,))],
    )
    return pl.pallas_call(
        functools.partial(_dispatch_body, n_exp, nbp, cap),
        grid_spec=grid_spec,
        out_shape=jax.ShapeDtypeStruct((n_exp, bn * cap + WIN, dx), BF16),
        compiler_params=pltpu.CompilerParams(dimension_semantics=("arbitrary", "arbitrary"),
                                             vmem_limit_bytes=VMEM_LIMIT_BYTES,
                                             has_side_effects=True),
        name="dispatch",
    )(lo_ext, hx, code_e)


def _ffn_body(d, nf, with_ctx, *refs):
    if with_ctx:
        x_ref, xc_ref, wg_ref, wu_ref, wd_ref, y_ref, yc_ref, acc_ref, accc_ref = refs
    else:
        x_ref, wg_ref, wu_ref, wd_ref, y_ref, acc_ref = refs
    e = pl.program_id(0)
    mi = pl.program_id(1)
    f = pl.program_id(2)
    wg = wg_ref[0, 0].astype(BF16)
    wu = wu_ref[0, 0].astype(BF16)
    wd = wd_ref[0, 0].astype(BF16)

    def run(xr, yr, acc):
        x = xr[0, :, :d]
        a = jnp.dot(x, wg, preferred_element_type=F32)
        u = jnp.dot(x, wu, preferred_element_type=F32)
        part = jnp.dot((_silu(a) * u).astype(BF16), wd, preferred_element_type=F32)

        @pl.when(f == 0)
        def _():
            acc[...] = part

        @pl.when(f > 0)
        def _():
            acc[...] += part

        @pl.when(f == nf - 1)
        def _():
            aff = xr[0, :, d:d + LANES].astype(F32) + xr[0, :, d + LANES:].astype(F32)
            lane = lax.broadcasted_iota(I32, aff.shape, 1)
            gate = jnp.sum(jnp.where(lane == e, aff, 0.0), axis=-1, keepdims=True)
            yr[0] = (acc[...] * gate).astype(yr.dtype)

    run(x_ref, y_ref, acc_ref)
    if with_ctx:
        @pl.when(mi == 0)
        def _():
            run(xc_ref, yc_ref, accc_ref)


def _ffn(xl, xc, w_gate, w_up, w_down, layer, tm, tmc, fc):
    n_exp, _, dx = xl.shape
    d = dx - 2 * LANES
    nm = (xl.shape[1] - WIN) // tm
    nf = w_gate.shape[-1] // fc
    with_ctx = xc is not None
    in_specs = [pl.BlockSpec((1, tm, dx), lambda e, mi, f: (e, mi, 0))]
    out_specs = [pl.BlockSpec((1, tm, d), lambda e, mi, f: (e, mi, 0))]
    out_shape = [jax.ShapeDtypeStruct((n_exp, nm * tm, d), BF16)]
    scratch = [pltpu.VMEM((tm, d), F32)]
    args = [xl]
    if with_ctx:
        in_specs.append(pl.BlockSpec((1, tmc, dx), lambda e, mi, f: (e, 0, 0)))
        out_specs.append(pl.BlockSpec((1, tmc, d), lambda e, mi, f: (e, 0, 0)))
        out_shape.append(jax.ShapeDtypeStruct((n_exp, tmc, d), BF16))
        scratch.append(pltpu.VMEM((tmc, d), F32))
        args.append(xc)
    in_specs += [pl.BlockSpec((1, 1, d, fc), lambda e, mi, f: (layer, e, 0, f)),
                 pl.BlockSpec((1, 1, d, fc), lambda e, mi, f: (layer, e, 0, f)),
                 pl.BlockSpec((1, 1, fc, d), lambda e, mi, f: (layer, e, f, 0))]
    outs = pl.pallas_call(
        functools.partial(_ffn_body, d, nf, with_ctx),
        grid=(n_exp, nm, nf),
        in_specs=in_specs,
        out_specs=out_specs,
        out_shape=out_shape,
        scratch_shapes=scratch,
        compiler_params=_cparams(("arbitrary", "arbitrary", "arbitrary")),
        name="ffn",
    )(*args, w_gate, w_up, w_down)
    return (outs[0], outs[1], nm * tm) if with_ctx else (outs[0], None, nm * tm)


def _combine_body(n_exp, nbp, cap, mtot, final, lo_ref, y_ref, x_ref, code_ref, m_ref,
                  gfin_ref, sel_ref, out_ref, ybuf, yextra, acc_ref, sem):
    b = pl.program_id(0)
    i = pl.program_id(1)
    nt = pl.num_programs(1)
    step = b * nt + i
    slot = step % 2
    tt, d = acc_ref.shape
    bpt = tt // LANES
    base = b * cap

    def tile_windows(bb, ii):
        firsts, npass = [], jnp.int32(0)
        for e in range(n_exp):
            o = (bb * n_exp + e) * nbp + ii * bpt
            a_e = bb * cap + ((lo_ref[o] >> ROW_SHIFT) << ROW_SHIFT)
            firsts.append(a_e)
            npass = jnp.maximum(npass, (bb * cap + lo_ref[o + bpt] - a_e + WIN - 1) >> WIN_SHIFT)
        return firsts, npass

    def clamp(w):
        return pl.multiple_of(jnp.minimum(w, mtot - WIN), ROW_ALIGN)

    def win_copy(e, row, buf, k):
        return pltpu.make_async_copy(y_ref.at[e, pl.ds(row, WIN)], buf.at[e], sem.at[k])

    starts, npass = tile_windows(b, i)

    @pl.when(step == 0)
    def _():
        for e in range(n_exp):
            win_copy(e, clamp(starts[e]), ybuf.at[0], 0).start()

    @pl.when(step + 1 < pl.num_programs(0) * nt)
    def _():
        wrap = i + 1 == nt
        nxt, _ = tile_windows(jnp.where(wrap, b + 1, b), jnp.where(wrap, 0, i + 1))
        for e in range(n_exp):
            win_copy(e, clamp(nxt[e]), ybuf.at[1 - slot], 1 - slot).start()

    acc_ref[...] = jnp.zeros_like(acc_ref)
    c1 = code_ref[0] + 1
    c_hi = (c1 >> CODE_SPLIT_SHIFT).astype(F32).astype(BF16)
    c_lo = (c1 & ((1 << CODE_SPLIT_SHIFT) - 1)).astype(F32).astype(BF16)
    spread = (jnp.dot(c_hi, sel_ref[...], preferred_element_type=F32) * float(1 << CODE_SPLIT_SHIFT)
              + jnp.dot(c_lo, sel_ref[...], preferred_element_type=F32))
    lane = lax.broadcasted_iota(I32, (1, n_exp * WIN), 1)
    lane_e = lane >> WIN_SHIFT
    lane_j = lane & (WIN - 1)

    def per_expert_row(vals):
        row = jnp.zeros_like(lane)
        for e in range(n_exp):
            row = jnp.where(lane_e == e, vals[e], row)
        return row

    def expand(buf, want, rows):
        slot_rank = (per_expert_row([r - base for r in rows]) + lane_j + 1).astype(F32)
        first_rank = (per_expert_row([w - base for w in want]) + 1).astype(F32)
        hit = (spread == slot_rank) & (spread >= first_rank)
        p_mat = jnp.where(hit, 1.0, 0.0).astype(BF16)
        acc_ref[...] += jnp.dot(p_mat, buf[...].reshape(n_exp * WIN, d), preferred_element_type=F32)

    rows0 = [clamp(a) for a in starts]
    for e in range(n_exp):
        win_copy(e, rows0[e], ybuf.at[slot], slot).wait()
    expand(ybuf.at[slot], starts, rows0)

    def later_pass(p, carry):
        want = [a + p * WIN for a in starts]
        rows = [clamp(w) for w in want]
        for e in range(n_exp):
            win_copy(e, rows[e], yextra, 2).start()
        for e in range(n_exp):
            win_copy(e, rows[e], yextra, 2).wait()
        expand(yextra, want, rows)
        return carry

    lax.fori_loop(1, npass, later_pass, 0)

    m = m_ref[0]
    lat_new = x_ref[0] + m[5:6] * acc_ref[...]
    if final:
        out_ref[0] = _rms(lat_new, gfin_ref[...])
    else:
        out_ref[0] = lat_new


def _combine(y, mtot, x, code_t, lo_ext, nbp, m, mrow, gfin, cap, final, tt):
    bn, t, d = x.shape
    n_exp = y.shape[0]
    assert cap >> CODE_SPLIT_SHIFT <= 256
    sel = (jnp.arange(LANES)[:, None] == jnp.arange(n_exp * WIN)[None, :] // WIN).astype(BF16)
    grid_spec = pltpu.PrefetchScalarGridSpec(
        num_scalar_prefetch=1,
        grid=(bn, t // tt),
        in_specs=[pl.BlockSpec(memory_space=pl.ANY),
                  pl.BlockSpec((1, tt, d), lambda b, i, lo: (b, i, 0)),
                  pl.BlockSpec((1, tt, LANES), lambda b, i, lo: (b, i, 0)),
                  pl.BlockSpec((1, 6, d), lambda b, i, lo: (mrow(b), 0, 0)),
                  pl.BlockSpec((1, d), lambda b, i, lo: (0, 0)),
                  pl.BlockSpec((LANES, n_exp * WIN), lambda b, i, lo: (0, 0))],
        out_specs=pl.BlockSpec((1, tt, d), lambda b, i, lo: (b, i, 0)),
        scratch_shapes=[pltpu.VMEM((2, n_exp, WIN, d), BF16), pltpu.VMEM((n_exp, WIN, d), BF16),
                        pltpu.VMEM((tt, d), F32), pltpu.SemaphoreType.DMA((3,))],
    )
    return pl.pallas_call(
        functools.partial(_combine_body, n_exp, nbp, cap, mtot, final),
        grid_spec=grid_spec,
        out_shape=jax.ShapeDtypeStruct((bn, t, d), F32),
        compiler_params=_cparams(("arbitrary", "arbitrary")),
        name="combine",
    )(lo_ext, y, x, code_t, m, gfin, sel)


def _moe(hx_l, aff_l, lat_mid, hx_c, aff_c, ctx_mid, m, layer, wg, wu, wd, final_g, last, tt_l, tt_c):
    bn, t, _ = hx_l.shape
    n_exp = aff_l.shape[1]
    with_ctx = hx_c is not None
    cap_l = CAPACITY_FACTOR * t // n_exp
    code_e_l, code_t_l, lo_l, nbp_l = _route(aff_l, cap_l)
    xl = _dispatch(hx_l, code_e_l, lo_l, nbp_l, cap_l, tt_l)
    xc, cap_c = None, 0
    if with_ctx:
        cap_c = CAPACITY_FACTOR * hx_c.shape[1] // n_exp
        code_e_c, code_t_c, lo_c, nbp_c = _route(aff_c, cap_c)
        xc = _dispatch(hx_c, code_e_c, lo_c, nbp_c, cap_c, tt_c)
    fc = min(1024, wg.shape[-1])
    yl, yc, rows_l = _ffn(xl, xc, wg, wu, wd, layer, cap_l, bn * cap_c, fc)
    lat_new = _combine(yl, rows_l, lat_mid, code_t_l, lo_l, nbp_l, m, lambda b: layer * 8 + b,
                       final_g, cap_l, last, tt_l)
    ctx_new = None
    if with_ctx:
        ctx_new = _combine(yc, bn * cap_c, ctx_mid, code_t_c, lo_c, nbp_c, m,
                           lambda b: layer * 8 + bn, final_g, cap_c, False, tt_c)
    return lat_new, ctx_new


def kernel(x, c, ctx, c_ctx, mod_w, mod_b, norm_mix_g, norm_ffn_g, cv_w1, cv_b1, cv_dw, cv_dwb,
           cv_ln_g, cv_ln_b, cv_w2, cv_b2, hg_w_in, hg_lb_logits, hg_norm_g, hg_w_out,
           router_w, exp_w_gate, exp_w_up, exp_w_down, final_g):
    bn, t, d = x.shape
    tc = ctx.shape[1]
    depth = mod_w.shape[0]
    n_exp = router_w.shape[-1]
    assert bn < 8 and d % HEAD == 0 and n_exp % 4 == 0 and n_exp <= LANES

    cond8 = jnp.concatenate([c, c_ctx[None], jnp.zeros((8 - bn - 1, d), F32)], axis=0)
    m = _mod(cond8, mod_w, mod_b).reshape(depth * 8, 6, d)

    tt_l = min(512, t)
    tt_c = min(256, tc)
    tmoe_l = min(256, t)
    tmoe_c = min(256, tc)
    row = lambda a: a.reshape(1, -1)
    fin = row(final_g)

    lat, cx = x, ctx
    for i in range(depth):
        last = i == depth - 1
        j = i // 2
        lat_row = lambda b, i=i: i * 8 + b
        ctx_row = lambda b, i=i: i * 8 + bn
        gmix, gffn = row(norm_mix_g[i]), row(norm_ffn_g[i])
        rw_pad = jnp.pad(router_w[i], ((0, 0), (0, LANES - n_exp)))
        rwh = rw_pad.astype(BF16)
        rww = jnp.concatenate([rwh, (rw_pad - rwh.astype(F32)).astype(BF16)], axis=1)
        need_ctx = not last
        hx_c = aff_c = ctx_mid = None
        if i % 2 == 0:
            w1 = cv_w1[j].astype(BF16)
            w2 = cv_w2[j].astype(BF16)
            cargs = (cv_dw[j], row(cv_dwb[j]), row(cv_ln_g[j]), row(cv_ln_b[j]), w2, row(cv_b2[j]),
                     gffn, rww, rwh, n_exp)
            u = _glu(lat, m, lat_row, gmix, w1, row(cv_b1[j]), tt_l)
            lat_mid, hx_l, aff_l = _conv(u, lat, m, lat_row, *cargs, tt_l)
            if need_ctx:
                uc = _glu(cx, m, ctx_row, gmix, w1, row(cv_b1[j]), tt_c)
                ctx_mid, hx_c, aff_c = _conv(uc, cx, m, ctx_row, *cargs, tt_c)
        else:
            w_in = hg_w_in[j].astype(BF16)
            w_out = hg_w_out[j].astype(BF16)
            ng = row(jnp.tile(hg_norm_g[j], d // HEAD))
            s_zero = jnp.zeros((bn, d // HEAD, HEAD, HEAD), F32)
            qc, kfc, kbc, lfc, lbc, vc, gc = _proj(cx, m, ctx_row, gmix, w_in, hg_lb_logits, i, tt_c)
            ql, kfl, kbl, lfl, lbl, vl, gl = _proj(lat, m, lat_row, gmix, w_in, hg_lb_logits, i, tt_l)
            tq_c, tq_l = min(256, tc), min(256, t)
            oc_f, oc_b, sc_f, sc_b = _scan(qc, kfc, kbc, vc, lfc, lbc, s_zero, s_zero, tq_c)
            ol_f, ol_b, _, _ = _scan(ql, kfl, kbl, vl, lfl, lbl, sc_f, sc_b, tq_l)
            pargs = (ng, w_out, gffn, rww, rwh, n_exp)
            lat_mid, hx_l, aff_l = _hgpost(ol_f, ol_b, gl, lat, m, lat_row, *pargs, tt_l)
            if need_ctx:
                ctx_mid, hx_c, aff_c = _hgpost(oc_f, oc_b, gc, cx, m, ctx_row, *pargs, tt_c)
        lat, cx_new = _moe(hx_l, aff_l, lat_mid, hx_c, aff_c, ctx_mid, m, i, exp_w_gate, exp_w_up,
                           exp_w_down, fin, last, tmoe_l, tmoe_c)
        if need_ctx:
            cx = cx_new
    return lat
```

```python
import functools

import jax
import jax.numpy as jnp
from jax import lax
from jax.experimental import pallas as pl
from jax.experimental.pallas import tpu as pltpu

F32 = jnp.float32
BF16 = jnp.bfloat16
I32 = jnp.int32
EPS = 1e-6
HIGHEST = lax.Precision.HIGHEST

LANES = 128
SUBLANES = 8
HEAD = 128
CHUNK = 128
HALF = CHUNK // 2
EXP_RANGE = 80.0
CAPACITY_FACTOR = 2
HALO = 16
WIN = 64
WIN_SHIFT = 6
ROW_ALIGN = 16
ROW_SHIFT = 4
REFINE_STEPS = 26
CODE_SPLIT_SHIFT = 5
VMEM_LIMIT_BYTES = 56 * 1024 * 1024

CONTRACT_LAST = (((1,), (1,)), ((), ()))
CONTRACT_FIRST = (((0,), (0,)), ((), ()))


def _cparams(sem):
    return pltpu.CompilerParams(dimension_semantics=sem, vmem_limit_bytes=VMEM_LIMIT_BYTES)


def _silu(x):
    return x * jax.nn.sigmoid(x)


def _rms(x, g):
    ms = jnp.mean(x * x, axis=-1, keepdims=True)
    return x * lax.rsqrt(ms + EPS) * g


def _prenorm(x, g, shift, scale):
    return _rms(x, g) * (1.0 + scale) + shift


def _split_bf16(x):
    hi = x.astype(BF16)
    return hi, (x - hi.astype(F32)).astype(BF16)


def _mod_body(c_ref, w_ref, b_ref, o_ref):
    c = c_ref[...]
    o_ref[0] = jnp.dot(_silu(c), w_ref[0], precision=HIGHEST,
                       preferred_element_type=F32) + b_ref[0]


def _mod(cond8, mod_w, mod_b):
    depth, d, d6 = mod_w.shape
    return pl.pallas_call(
        _mod_body,
        grid=(depth, d6 // d),
        in_specs=[pl.BlockSpec((8, d), lambda l, j: (0, 0)),
                  pl.BlockSpec((1, d, d), lambda l, j: (l, 0, j)),
                  pl.BlockSpec((1, 1, d), lambda l, j: (l, 0, j))],
        out_specs=pl.BlockSpec((1, 8, d), lambda l, j: (l, 0, j)),
        out_shape=jax.ShapeDtypeStruct((depth, 8, d6), F32),
        compiler_params=_cparams(("arbitrary", "arbitrary")),
        name="mod",
    )(cond8, mod_w, mod_b.reshape(depth, 1, d6))


def _post(lat, y, m, gffn, rww_ref, rwh_ref, n_exp, lat_o, hx_o, aff_o):
    d = lat.shape[-1]
    lat_new = lat + m[2:3] * y
    h = _prenorm(lat_new, gffn, m[3:4], m[4:5])
    h_hi, h_lo = _split_bf16(h)
    wide = jnp.dot(h_hi, rww_ref[...], preferred_element_type=F32)
    logits = (wide[:, :LANES] + wide[:, LANES:]
              + jnp.dot(h_lo, rwh_ref[...], preferred_element_type=F32))
    lane = lax.broadcasted_iota(I32, logits.shape, 1)
    lg = jnp.where(lane < n_exp, logits, -jnp.inf)
    ex = jnp.exp(lg - jnp.max(lg, axis=-1, keepdims=True))
    aff = ex / jnp.sum(ex, axis=-1, keepdims=True)
    a_hi, a_lo = _split_bf16(aff)
    lat_o[0] = lat_new
    hx_o[0, :, :d] = h_hi
    hx_o[0, :, d:d + LANES] = a_hi
    hx_o[0, :, d + LANES:] = a_lo
    aff_o[0] = aff.T[:n_exp]


def _post_specs(bn, t, d, n_exp, tt):
    specs = [pl.BlockSpec((1, tt, d), lambda b, i: (b, i, 0)),
             pl.BlockSpec((1, tt, d + 2 * LANES), lambda b, i: (b, i, 0)),
             pl.BlockSpec((1, n_exp, tt), lambda b, i: (b, 0, i))]
    shapes = [jax.ShapeDtypeStruct((bn, t, d), F32),
              jax.ShapeDtypeStruct((bn, t, d + 2 * LANES), BF16),
              jax.ShapeDtypeStruct((bn, n_exp, t), F32)]
    return specs, shapes


def _glu_body(x_ref, m_ref, g_ref, w1_ref, b1_ref, u_ref):
    d = x_ref.shape[-1]
    m = m_ref[0]
    h = _prenorm(x_ref[0], g_ref[...], m[0:1], m[1:2]).astype(BF16)
    a = jnp.dot(h, w1_ref[:, :d], preferred_element_type=F32) + b1_ref[:, :d]
    gate = jnp.dot(h, w1_ref[:, d:], preferred_element_type=F32) + b1_ref[:, d:]
    u_ref[0] = (a * jax.nn.sigmoid(gate)).astype(u_ref.dtype)


def _glu(x, m, mrow, g, w1, b1, tt):
    bn, t, d = x.shape
    return pl.pallas_call(
        _glu_body,
        grid=(bn, t // tt),
        in_specs=[pl.BlockSpec((1, tt, d), lambda b, i: (b, i, 0)),
                  pl.BlockSpec((1, 6, d), lambda b, i: (mrow(b), 0, 0)),
                  pl.BlockSpec((1, d), lambda b, i: (0, 0)),
                  pl.BlockSpec((d, 2 * d), lambda b, i: (0, 0)),
                  pl.BlockSpec((1, 2 * d), lambda b, i: (0, 0))],
        out_specs=pl.BlockSpec((1, tt, d), lambda b, i: (b, i, 0)),
        out_shape=jax.ShapeDtypeStruct((bn, t, d), BF16),
        compiler_params=_cparams(("arbitrary", "arbitrary")),
        name="glu",
    )(x, m, g, w1, b1)


def _conv_body(nt, kw, n_exp, up_ref, uc_ref, un_ref, x_ref, m_ref, dw_ref, dwb_ref, lng_ref,
               lnb_ref, w2_ref, b2_ref, gffn_ref, rww_ref, rwh_ref, lat_o, hx_o, aff_o,
               xs_ref, sh_ref, cv_ref):
    i = pl.program_id(1)
    tt, d = cv_ref.shape
    pad = (kw - 1) // 2
    xs_ref[0:HALO, :] = jnp.where(i > 0, up_ref[0].astype(F32), 0.0)
    xs_ref[HALO:HALO + tt, :] = uc_ref[0].astype(F32)
    xs_ref[HALO + tt:, :] = jnp.where(i < nt - 1, un_ref[0].astype(F32), 0.0)

    strip = min(tt, 128)
    span = sh_ref.shape[1]

    def chan_block(j, carry):
        c0 = pl.multiple_of(j * LANES, LANES)
        for s in range(SUBLANES):
            sh_ref[s] = xs_ref[s:s + span, pl.ds(c0, LANES)]
        for st in range(tt // strip):
            acc = jnp.zeros((strip, LANES), F32)
            for k in range(kw):
                off = k + HALO - pad
                r0 = st * strip + (off // SUBLANES) * SUBLANES
                acc = acc + sh_ref[off % SUBLANES, r0:r0 + strip, :] * dw_ref[k:k + 1, pl.ds(c0, LANES)]
            cv_ref[st * strip:(st + 1) * strip, pl.ds(c0, LANES)] = acc + dwb_ref[:, pl.ds(c0, LANES)]
        return carry

    lax.fori_loop(0, d // LANES, chan_block, 0)

    cv = cv_ref[...]
    mu = jnp.mean(cv, axis=-1, keepdims=True)
    xc = cv - mu
    var = jnp.mean(xc * xc, axis=-1, keepdims=True)
    y = xc * lax.rsqrt(var + EPS) * lng_ref[...] + lnb_ref[...]
    z = jnp.dot(_silu(y).astype(BF16), w2_ref[...], preferred_element_type=F32) + b2_ref[...]
    _post(x_ref[0], z, m_ref[0], gffn_ref[...], rww_ref, rwh_ref, n_exp, lat_o, hx_o, aff_o)


def _conv(u, x, m, mrow, dw, dwb, lng, lnb, w2, b2, gffn, rww, rwh, n_exp, tt):
    bn, t, d = x.shape
    nt = t // tt
    kw = dw.shape[0]
    assert (kw - 1) // 2 <= HALO
    hb = tt // HALO
    nhb = t // HALO
    row = lambda b, i: (0, 0)
    out_specs, out_shape = _post_specs(bn, t, d, n_exp, tt)
    return pl.pallas_call(
        functools.partial(_conv_body, nt, kw, n_exp),
        grid=(bn, nt),
        in_specs=[pl.BlockSpec((1, HALO, d), lambda b, i: (b, jnp.maximum(i * hb - 1, 0), 0)),
                  pl.BlockSpec((1, tt, d), lambda b, i: (b, i, 0)),
                  pl.BlockSpec((1, HALO, d), lambda b, i: (b, jnp.minimum((i + 1) * hb, nhb - 1), 0)),
                  pl.BlockSpec((1, tt, d), lambda b, i: (b, i, 0)),
                  pl.BlockSpec((1, 6, d), lambda b, i: (mrow(b), 0, 0)),
                  pl.BlockSpec((kw, d), row),
                  pl.BlockSpec((1, d), row), pl.BlockSpec((1, d), row), pl.BlockSpec((1, d), row),
                  pl.BlockSpec((d, d), row), pl.BlockSpec((1, d), row), pl.BlockSpec((1, d), row),
                  pl.BlockSpec((d, 2 * LANES), row), pl.BlockSpec((d, LANES), row)],
        out_specs=out_specs,
        out_shape=out_shape,
        scratch_shapes=[pltpu.VMEM((tt + 2 * HALO, d), F32),
                        pltpu.VMEM((SUBLANES, tt + 2 * HALO - SUBLANES, LANES), F32),
                        pltpu.VMEM((tt, d), F32)],
        compiler_params=_cparams(("arbitrary", "arbitrary")),
        name="conv",
    )(u, u, u, x, m, dw, dwb, lng, lnb, w2, b2, gffn, rww, rwh)


def _proj_body(layer, x_ref, m_ref, g_ref, w_ref, lbl_ref, q_o, kf_o, kb_o, lf_o, lb_o, v_o, g_o):
    d = x_ref.shape[-1]
    m = m_ref[0]
    h = _prenorm(x_ref[0], g_ref[...], m[0:1], m[1:2]).astype(BF16)
    lg = lbl_ref[...]
    ex = jnp.exp(lg - jnp.max(lg, axis=0, keepdims=True))
    soft = ex / jnp.sum(ex, axis=0, keepdims=True)
    lbd = jnp.zeros(soft.shape[1:], F32)
    for l in range(1, layer + 1):
        lbd = lbd + soft[l]

    def col(j):
        return jnp.dot(h, w_ref[:, j * d:(j + 1) * d], preferred_element_type=F32)

    q_o[0] = col(0).astype(q_o.dtype)
    for dr, (k_o, l_o) in enumerate(((kf_o, lf_o), (kb_o, lb_o))):
        lb_row = lbd[dr:dr + 1]
        fg = lb_row + (1.0 - lb_row) * jax.nn.sigmoid(col(1 + dr))
        k_o[0] = (1.0 - fg).astype(k_o.dtype)
        l_o[0] = jnp.log(fg)
    v_o[0] = col(3).astype(v_o.dtype)
    g_o[0] = col(4).astype(g_o.dtype)


def _proj(x, m, mrow, g, w_in, lb_logits, layer, tt):
    bn, t, d = x.shape
    blk = pl.BlockSpec((1, tt, d), lambda b, i: (b, i, 0))
    sd = lambda dt: jax.ShapeDtypeStruct((bn, t, d), dt)
    return pl.pallas_call(
        functools.partial(_proj_body, layer),
        grid=(bn, t // tt),
        in_specs=[blk,
                  pl.BlockSpec((1, 6, d), lambda b, i: (mrow(b), 0, 0)),
                  pl.BlockSpec((1, d), lambda b, i: (0, 0)),
                  pl.BlockSpec((d, 5 * d), lambda b, i: (0, 0)),
                  pl.BlockSpec(lb_logits.shape, lambda b, i: (0, 0, 0))],
        out_specs=[blk] * 7,
        out_shape=[sd(BF16), sd(BF16), sd(BF16), sd(F32), sd(F32), sd(BF16), sd(BF16)],
        compiler_params=_cparams(("arbitrary", "arbitrary")),
        name="hgproj",
    )(x, m, g, w_in, lb_logits)


def _scan_chunk(reverse, q_ref, k_ref, v_ref, lf_ref, o_ref, st_ref, tmp_ref, rows):
    d = q_ref.shape[2]
    r = lax.broadcasted_iota(I32, (CHUNK, CHUNK), 0)
    c = lax.broadcasted_iota(I32, (CHUNK, CHUNK), 1)
    tri = ((c >= r) if reverse else (c <= r)).astype(BF16)
    rh = lax.broadcasted_iota(I32, (HALF, HALF), 0)
    ch = lax.broadcasted_iota(I32, (HALF, HALF), 1)
    keep_half = (ch >= rh) if reverse else (ch <= rh)
    tot_row = 0 if reverse else CHUNK - 1

    lf = lf_ref[0, rows, :]
    lf_hi, lf_lo = _split_bf16(lf)
    g = (jnp.dot(tri, lf_hi, preferred_element_type=F32)
         + jnp.dot(tri, lf_lo, preferred_element_type=F32))
    gtot = g[tot_row:tot_row + 1]
    q = q_ref[0, rows, :].astype(F32)
    k = k_ref[0, rows, :].astype(F32)
    v = v_ref[0, rows, :]
    q_in = (q * jnp.exp(g)).astype(BF16)
    k_st = (k * jnp.exp(gtot - g)).astype(BF16)
    d_st = jnp.exp(gtot)

    halves = (slice(0, HALF), slice(HALF, CHUNK))

    def max_abs(x):
        return jnp.max(jnp.max(jnp.abs(x), axis=0, keepdims=True), axis=1, keepdims=True)[0, 0]

    def centre_and_reach(rows_):
        lo_row, hi_row = g[rows_.start:rows_.start + 1], g[rows_.stop - 1:rows_.stop]
        return 0.5 * (lo_row + hi_row), max_abs(0.5 * (hi_row - lo_row))

    gmid, reach = centre_and_reach(slice(0, CHUNK))
    half_refs, half_reach = zip(*[centre_and_reach(hs) for hs in halves])
    whole_ok = reach <= EXP_RANGE
    halves_ok = jnp.maximum(half_reach[0], half_reach[1]) <= EXP_RANGE

    def inter_and_state(h, sl):
        st = st_ref[h]
        o = lax.dot_general(q_in[:, sl], st.astype(BF16), CONTRACT_LAST, preferred_element_type=F32)
        st_ref[h] = st * d_st[:, sl] + lax.dot_general(
            v[:, sl], k_st[:, sl], CONTRACT_FIRST, preferred_element_type=F32)
        return o

    @pl.when(whole_ok)
    def _():
        keep = (c >= r) if reverse else (c <= r)
        q_mid = (q * jnp.exp(g - gmid)).astype(BF16)
        k_mid = (k * jnp.exp(gmid - g)).astype(BF16)
        for h in range(d // HEAD):
            sl = slice(h * HEAD, (h + 1) * HEAD)
            sc = lax.dot_general(q_mid[:, sl], k_mid[:, sl], CONTRACT_LAST,
                                 preferred_element_type=F32)
            sc = jnp.where(keep, sc, 0.0).astype(BF16)
            o_ref[0, rows, sl] = (inter_and_state(h, sl)
                                  + jnp.dot(sc, v[:, sl], preferred_element_type=F32)
                                  ).astype(o_ref.dtype)

    @pl.when(jnp.logical_not(whole_ok) & halves_ok)
    def _():
        q_dg = [(q[hs] * jnp.exp(g[hs] - ref)).astype(BF16) for hs, ref in zip(halves, half_refs)]
        k_dg = [(k[hs] * jnp.exp(ref - g[hs])).astype(BF16) for hs, ref in zip(halves, half_refs)]
        qs, ks = (halves[0], halves[1]) if reverse else (halves[1], halves[0])
        edge = g[HALF:HALF + 1] if reverse else g[HALF - 1:HALF]
        q_x = (q[qs] * jnp.exp(g[qs] - edge)).astype(BF16)
        k_x = (k[ks] * jnp.exp(edge - g[ks])).astype(BF16)
        seen = 1 if reverse else 0
        for h in range(d // HEAD):
            sl = slice(h * HEAD, (h + 1) * HEAD)
            diag = [jnp.where(keep_half,
                              lax.dot_general(q_dg[i][:, sl], k_dg[i][:, sl], CONTRACT_LAST,
                                              preferred_element_type=F32), 0.0) for i in range(2)]
            cross = lax.dot_general(q_x[:, sl], k_x[:, sl], CONTRACT_LAST,
                                    preferred_element_type=F32)
            vh = [v[hs, sl] for hs in halves]
            intra = [jnp.dot(diag[i].astype(BF16), vh[i], preferred_element_type=F32)
                     for i in range(2)]
            intra[1 - seen] = intra[1 - seen] + jnp.dot(cross.astype(BF16), vh[seen],
                                                        preferred_element_type=F32)
            o = inter_and_state(h, sl)
            for i, hs in enumerate(halves):
                o_ref[0, slice(rows.start + hs.start, rows.start + hs.stop), sl] = (
                    o[hs] + intra[i]).astype(o_ref.dtype)

    @pl.when(jnp.logical_not(whole_ok | halves_ok))
    def _():
        row_id = lax.broadcasted_iota(I32, (ROW_ALIGN, d), 0)
        ngroup = CHUNK // ROW_ALIGN

        def group(gi, carry):
            gidx = (ngroup - 1 - gi) if reverse else gi
            r0 = pl.multiple_of(rows.start + gidx * ROW_ALIGN, ROW_ALIGN)
            grp = pl.ds(r0, ROW_ALIGN)
            tmp_ref[0] = q_ref[0, grp, :].astype(F32)
            tmp_ref[1] = k_ref[0, grp, :].astype(F32)
            tmp_ref[2] = v_ref[0, grp, :].astype(F32)
            tmp_ref[3] = jnp.exp(lf_ref[0, grp, :])
            tmp_ref[4] = jnp.zeros((ROW_ALIGN, d), F32)

            def token(ti, carry2):
                t = (ROW_ALIGN - 1 - ti) if reverse else ti
                spread_row = lambda j: jnp.broadcast_to(tmp_ref[j, pl.ds(t, 1), :], (ROW_ALIGN, d))
                only_first = lambda a: jnp.where(row_id == 0, a, 0.0).astype(BF16)
                q_all = spread_row(0).astype(BF16)
                k_one, v_one = only_first(spread_row(1)), only_first(spread_row(2))
                f_row = tmp_ref[3, pl.ds(t, 1), :]
                pieces = []
                for h in range(d // HEAD):
                    sl = slice(h * HEAD, (h + 1) * HEAD)
                    st = st_ref[h] * f_row[:, sl] + lax.dot_general(
                        v_one[:, sl], k_one[:, sl], CONTRACT_FIRST, preferred_element_type=F32)
                    st_ref[h] = st
                    pieces.append(lax.dot_general(q_all[:, sl], st.astype(BF16), CONTRACT_LAST,
                                                  preferred_element_type=F32))
                tmp_ref[4] = jnp.where(row_id == t, jnp.concatenate(pieces, axis=1), tmp_ref[4])
                return carry2

            lax.fori_loop(0, ROW_ALIGN, token, 0)
            o_ref[0, grp, :] = tmp_ref[4].astype(o_ref.dtype)
            return carry

        lax.fori_loop(0, ngroup, group, 0)


def _scan_body(nsteps, qf_ref, kf_ref, vf_ref, lf_ref, qb_ref, kb_ref, vb_ref, lb_ref, s0f_ref,
               s0b_ref, of_ref, ob_ref, sff_ref, sfb_ref, stf_ref, stb_ref, tmp_ref):
    step = pl.program_id(1)
    nchunk = qf_ref.shape[1] // CHUNK

    @pl.when(step == 0)
    def _():
        stf_ref[...] = s0f_ref[0]
        stb_ref[...] = s0b_ref[0]

    for ci in range(nchunk):
        cf, cb = ci, nchunk - 1 - ci
        _scan_chunk(False, qf_ref, kf_ref, vf_ref, lf_ref, of_ref, stf_ref, tmp_ref,
                    slice(cf * CHUNK, (cf + 1) * CHUNK))
        _scan_chunk(True, qb_ref, kb_ref, vb_ref, lb_ref, ob_ref, stb_ref, tmp_ref,
                    slice(cb * CHUNK, (cb + 1) * CHUNK))

    @pl.when(step == nsteps - 1)
    def _():
        sff_ref[0] = stf_ref[...]
        sfb_ref[0] = stb_ref[...]


def _scan(q, kf, kb, v, lf, lb, s0f, s0b, tq):
    bn, t, d = q.shape
    nh = d // HEAD
    nsteps = t // tq
    fblk = pl.BlockSpec((1, tq, d), lambda b, i: (b, i, 0))
    bblk = pl.BlockSpec((1, tq, d), lambda b, i: (b, nsteps - 1 - i, 0))
    sblk = pl.BlockSpec((1, nh, HEAD, HEAD), lambda b, i: (b, 0, 0, 0))
    st_shape = jax.ShapeDtypeStruct((bn, nh, HEAD, HEAD), F32)
    o_shape = jax.ShapeDtypeStruct((bn, t, d), BF16)
    return pl.pallas_call(
        functools.partial(_scan_body, nsteps),
        grid=(bn, nsteps),
        in_specs=[fblk] * 4 + [bblk] * 4 + [sblk, sblk],
        out_specs=[fblk, bblk, sblk, sblk],
        out_shape=[o_shape, o_shape, st_shape, st_shape],
        scratch_shapes=[pltpu.VMEM((nh, HEAD, HEAD), F32), pltpu.VMEM((nh, HEAD, HEAD), F32),
                        pltpu.VMEM((5, ROW_ALIGN, d), F32)],
        compiler_params=_cparams(("arbitrary", "arbitrary")),
        name="hgscan",
    )(q, kf, v, lf, q, kb, v, lb, s0f, s0b)


def _hgpost_body(n_exp, of_ref, ob_ref, g_ref, x_ref, m_ref, ng_ref, wo_ref, gffn_ref, rww_ref,
                 rwh_ref, lat_o, hx_o, aff_o):
    o = of_ref[0].astype(F32) + ob_ref[0].astype(F32)
    d = o.shape[-1]
    parts = []
    for h in range(d // HEAD):
        oh = o[:, h * HEAD:(h + 1) * HEAD]
        ms = jnp.mean(oh * oh, axis=-1, keepdims=True)
        parts.append(oh * lax.rsqrt(ms + EPS))
    on = jnp.concatenate(parts, axis=1) * ng_ref[...]
    y = (on * _silu(g_ref[0].astype(F32))).astype(BF16)
    z = jnp.dot(y, wo_ref[...], preferred_element_type=F32)
    _post(x_ref[0], z, m_ref[0], gffn_ref[...], rww_ref, rwh_ref, n_exp, lat_o, hx_o, aff_o)


def _hgpost(o_fw, o_bw, g, x, m, mrow, ng_tiled, w_out, gffn, rww, rwh, n_exp, tt):
    bn, t, d = x.shape
    blk = pl.BlockSpec((1, tt, d), lambda b, i: (b, i, 0))
    row = lambda b, i: (0, 0)
    out_specs, out_shape = _post_specs(bn, t, d, n_exp, tt)
    return pl.pallas_call(
        functools.partial(_hgpost_body, n_exp),
        grid=(bn, t // tt),
        in_specs=[blk, blk, blk, blk,
                  pl.BlockSpec((1, 6, d), lambda b, i: (mrow(b), 0, 0)),
                  pl.BlockSpec((1, d), row), pl.BlockSpec((d, d), row), pl.BlockSpec((1, d), row),
                  pl.BlockSpec((d, 2 * LANES), row), pl.BlockSpec((d, LANES), row)],
        out_specs=out_specs,
        out_shape=out_shape,
        compiler_params=_cparams(("arbitrary", "arbitrary")),
        name="hgpost",
    )(o_fw, o_bw, g, x, m, ng_tiled, w_out, gffn, rww, rwh)


def _route_body(n_exp, nb, k_sel, a_ref, code_e_o, code_t_o, lo_o, code_s):
    rows = n_exp * nb
    shift = nb.bit_length() - 1
    a3 = a_ref[0].reshape(n_exp, nb, LANES)

    def count(mask3):
        s = jnp.sum(jnp.where(mask3, 1.0, 0.0), axis=1, keepdims=True)
        return jnp.sum(s, axis=2, keepdims=True)

    thr = jnp.zeros((n_exp, 1, 1), I32)
    for bit in range(30, -1, -1):
        cand = thr | (1 << bit)
        enough = count(a3 >= lax.bitcast_convert_type(cand, F32)) >= k_sel
        thr = jnp.where(enough, cand, thr)
    lo = lax.bitcast_convert_type(thr, F32)
    hi = lax.bitcast_convert_type(thr + 1, F32)
    for _ in range(REFINE_STEPS):
        mid = 0.5 * (lo + hi)
        enough = count(a3 >= mid) >= k_sel
        lo = jnp.where(enough, mid, lo)
        hi = jnp.where(enough, hi, mid)
    gt3 = a3 >= hi
    eq3 = (a3 >= lo) & jnp.logical_not(gt3)
    need = k_sel - count(gt3)

    ri = lax.broadcasted_iota(I32, (rows, rows), 0)
    ci = lax.broadcasted_iota(I32, (rows, rows), 1)
    lblk = (((ri >> shift) == (ci >> shift)) & (ci < ri)).astype(BF16)
    ui = lax.broadcasted_iota(I32, (LANES, LANES), 0)
    uj = lax.broadcasted_iota(I32, (LANES, LANES), 1)
    upper = (ui <= uj).astype(BF16)
    ones = jnp.ones((LANES, LANES), BF16)

    def cumsum(x):
        xb = x.astype(BF16)
        within = jnp.dot(xb, upper, preferred_element_type=F32)
        part = jnp.dot(lblk, xb, preferred_element_type=F32)
        before = jnp.dot(part.astype(BF16), ones, preferred_element_type=F32)
        return within + before, before

    eqf = jnp.where(eq3, 1.0, 0.0).reshape(rows, LANES)
    eq_incl, _ = cumsum(eqf)
    eq_excl3 = (eq_incl - eqf).reshape(n_exp, nb, LANES)
    sel3 = gt3 | (eq3 & (eq_excl3 < need))
    self = jnp.where(sel3, 1.0, 0.0).reshape(rows, LANES)
    incl, before = cumsum(self)
    lo_o[0] = before.astype(I32)
    code = jnp.where(self > 0.0, incl - self, -1.0).astype(I32)
    code_e_o[0] = code
    code_s[...] = code

    fill = jnp.full((LANES - n_exp, LANES), -1, I32)
    for blk in range(nb):
        tile = code_s[pl.ds(blk, n_exp, stride=nb), :]
        code_t_o[0, blk * LANES:(blk + 1) * LANES, :] = jnp.concatenate([tile, fill], axis=0).T


def _route(aff, k_sel):
    bn, n_exp, t = aff.shape
    tp = max(t, SUBLANES * LANES)
    if tp != t:
        aff = jnp.pad(aff, ((0, 0), (0, 0), (0, tp - t)), constant_values=-1.0)
    nb = tp // LANES
    assert nb & (nb - 1) == 0
    rows = n_exp * nb
    blk = pl.BlockSpec((1, rows, LANES), lambda b: (b, 0, 0))
    code_e, code_t, lo = pl.pallas_call(
        functools.partial(_route_body, n_exp, nb, k_sel),
        grid=(bn,),
        in_specs=[blk],
        out_specs=[blk, pl.BlockSpec((1, tp, LANES), lambda b: (b, 0, 0)), blk],
        out_shape=[jax.ShapeDtypeStruct((bn, rows, LANES), I32),
                   jax.ShapeDtypeStruct((bn, tp, LANES), I32),
                   jax.ShapeDtypeStruct((bn, rows, LANES), I32)],
        scratch_shapes=[pltpu.VMEM((rows, LANES), I32)],
        compiler_params=_cparams(("arbitrary",)),
        name="route",
    )(aff.reshape(bn, rows, LANES))
    lo3 = lo[:, :, 0].reshape(bn, n_exp, nb)
    lo_ext = jnp.concatenate([lo3, jnp.full((bn, n_exp, 1), k_sel, I32)], axis=-1)
    return code_e.reshape(bn, n_exp, tp), code_t, lo_ext.reshape(-1), nb + 1


def _dispatch_body(n_exp, nbp, cap, lo_ref, hx_ref, code_ref, x_ref, xwin, carry, pend, sem):
    b = pl.program_id(0)
    i = pl.program_id(1)
    tt = hx_ref.shape[1]
    bpt = tt // LANES
    base = b * cap
    is_last = (b == pl.num_programs(0) - 1) & (i == pl.num_programs(1) - 1)

    @pl.when(i == 0)
    def _():
        carry[...] = jnp.zeros_like(carry)

    def out_copy(e, row):
        return pltpu.make_async_copy(xwin.at[e], x_ref.at[e, pl.ds(row, WIN)], sem.at[0])

    def drain():
        for e in range(n_exp):
            @pl.when(pend[e] == 1)
            def _(e=e):
                out_copy(e, 0).wait()
                pend[e] = 0

    @pl.when((b == 0) & (i == 0))
    def _():
        xwin[...] = jnp.zeros_like(xwin)
        for e in range(n_exp):
            out_copy(e, pl.num_programs(0) * cap).start()
            pend[e] = 1

    starts, next_blk = [], []
    npass = jnp.int32(0)
    for e in range(n_exp):
        o = (b * n_exp + e) * nbp + i * bpt
        a_e = (lo_ref[o] >> ROW_SHIFT) << ROW_SHIFT
        c_e = ((lo_ref[o + bpt] >> ROW_SHIFT) << ROW_SHIFT) - a_e
        starts.append(a_e)
        next_blk.append(c_e)
        npass = jnp.maximum(npass, (c_e >> WIN_SHIFT) + 1)

    h = hx_ref[0]
    code = code_ref[0]
    jrow = lax.broadcasted_iota(I32, (WIN, tt), 0)

    def one_pass(p, loop_carry):
        onehot = []
        for e in range(n_exp):
            rel = code[e:e + 1, :] - (starts[e] + p * WIN)
            onehot.append(jnp.where(rel == jrow, 1.0, 0.0).astype(BF16))
        xw = jnp.dot(jnp.concatenate(onehot, axis=0), h, preferred_element_type=F32)
        drain()
        for e in range(n_exp):
            xwin[e] = xw[e * WIN:(e + 1) * WIN].astype(BF16)

        @pl.when(p == 0)
        def _():
            for e in range(n_exp):
                xwin[e, 0:ROW_ALIGN, :] = xwin[e, 0:ROW_ALIGN, :] + carry[e]

        for e in range(n_exp):
            off = next_blk[e] - p * WIN

            @pl.when((off >= 0) & (off < WIN))
            def _(e=e, off=off):
                carry[e] = xwin[e, pl.ds(pl.multiple_of(off, ROW_ALIGN), ROW_ALIGN), :]

            @pl.when(off >= 0)
            def _(e=e):
                out_copy(e, pl.multiple_of(base + starts[e] + p * WIN, ROW_ALIGN)).start(
                    priority=e % 2)
                pend[e] = 1
        return loop_carry

    lax.fori_loop(0, npass, one_pass, 0)

    @pl.when(is_last)
    def _():
        drain()


def _dispatch(hx, code_e, lo_ext, nbp, cap, tt):
    bn, t, dx = hx.shape
    n_exp = code_e.shape[1]
    assert cap % ROW_ALIGN == 0
    grid_spec = pltpu.PrefetchScalarGridSpec(
        num_scalar_prefetch=1,
        grid=(bn, t // tt),
        in_specs=[pl.BlockSpec((1, tt, dx), lambda b, i, lo: (b, i, 0)),
                  pl.BlockSpec((1, n_exp, tt), lambda b, i, lo: (b, 0, i))],
        out_specs=pl.BlockSpec(memory_space=pl.ANY),
        scratch_shapes=[pltpu.VMEM((n_exp, WIN, dx), BF16), pltpu.VMEM((n_exp, ROW_ALIGN, dx), BF16),
                        pltpu.SMEM((n_exp,), I32), pltpu.SemaphoreType.DMA((1,))],
    )
    return pl.pallas_call(
        functools.partial(_dispatch_body, n_exp, nbp, cap),
        grid_spec=grid_spec,
        out_shape=jax.ShapeDtypeStruct((n_exp, bn * cap + WIN, dx), BF16),
        compiler_params=pltpu.CompilerParams(dimension_semantics=("arbitrary", "arbitrary"),
                                             vmem_limit_bytes=VMEM_LIMIT_BYTES,
                                             has_side_effects=True),
        name="dispatch",
    )(lo_ext, hx, code_e)


def _ffn_body(d, nf, with_ctx, *refs):
    if with_ctx:
        x_ref, xc_ref, wg_ref, wu_ref, wd_ref, y_ref, yc_ref, acc_ref, accc_ref = refs
    else:
        x_ref, wg_ref, wu_ref, wd_ref, y_ref, acc_ref = refs
    e = pl.program_id(0)
    mi = pl.program_id(1)
    f = pl.program_id(2)
    wg = wg_ref[0, 0].astype(BF16)
    wu = wu_ref[0, 0].astype(BF16)
    wd = wd_ref[0, 0].astype(BF16)

    def run(xr, yr, acc):
        x = xr[0, :, :d]
        a = jnp.dot(x, wg, preferred_element_type=F32)
        u = jnp.dot(x, wu, preferred_element_type=F32)
        part = jnp.dot((_silu(a) * u).astype(BF16), wd, preferred_element_type=F32)

        @pl.when(f == 0)
        def _():
            acc[...] = part

        @pl.when(f > 0)
        def _():
            acc[...] += part

        @pl.when(f == nf - 1)
        def _():
            aff = xr[0, :, d:d + LANES].astype(F32) + xr[0, :, d + LANES:].astype(F32)
            lane = lax.broadcasted_iota(I32, aff.shape, 1)
            gate = jnp.sum(jnp.where(lane == e, aff, 0.0), axis=-1, keepdims=True)
            yr[0] = (acc[...] * gate).astype(yr.dtype)

    run(x_ref, y_ref, acc_ref)
    if with_ctx:
        @pl.when(mi == 0)
        def _():
            run(xc_ref, yc_ref, accc_ref)


def _ffn(xl, xc, w_gate, w_up, w_down, layer, tm, tmc, fc):
    n_exp, _, dx = xl.shape
    d = dx - 2 * LANES
    nm = (xl.shape[1] - WIN) // tm
    nf = w_gate.shape[-1] // fc
    with_ctx = xc is not None
    in_specs = [pl.BlockSpec((1, tm, dx), lambda e, mi, f: (e, mi, 0))]
    out_specs = [pl.BlockSpec((1, tm, d), lambda e, mi, f: (e, mi, 0))]
    out_shape = [jax.ShapeDtypeStruct((n_exp, nm * tm, d), BF16)]
    scratch = [pltpu.VMEM((tm, d), F32)]
    args = [xl]
    if with_ctx:
        in_specs.append(pl.BlockSpec((1, tmc, dx), lambda e, mi, f: (e, 0, 0)))
        out_specs.append(pl.BlockSpec((1, tmc, d), lambda e, mi, f: (e, 0, 0)))
        out_shape.append(jax.ShapeDtypeStruct((n_exp, tmc, d), BF16))
        scratch.append(pltpu.VMEM((tmc, d), F32))
        args.append(xc)
    in_specs += [pl.BlockSpec((1, 1, d, fc), lambda e, mi, f: (layer, e, 0, f)),
                 pl.BlockSpec((1, 1, d, fc), lambda e, mi, f: (layer, e, 0, f)),
                 pl.BlockSpec((1, 1, fc, d), lambda e, mi, f: (layer, e, f, 0))]
    outs = pl.pallas_call(
        functools.partial(_ffn_body, d, nf, with_ctx),
        grid=(n_exp, nm, nf),
        in_specs=in_specs,
        out_specs=out_specs,
        out_shape=out_shape,
        scratch_shapes=scratch,
        compiler_params=_cparams(("arbitrary", "arbitrary", "arbitrary")),
        name="ffn",
    )(*args, w_gate, w_up, w_down)
    return (outs[0], outs[1], nm * tm) if with_ctx else (outs[0], None, nm * tm)


def _combine_body(n_exp, nbp, cap, mtot, final, lo_ref, y_ref, x_ref, code_ref, m_ref,
                  gfin_ref, sel_ref, out_ref, ybuf, yextra, acc_ref, sem):
    b = pl.program_id(0)
    i = pl.program_id(1)
    nt = pl.num_programs(1)
    step = b * nt + i
    slot = step % 2
    tt, d = acc_ref.shape
    bpt = tt // LANES
    base = b * cap

    def tile_windows(bb, ii):
        firsts, npass = [], jnp.int32(0)
        for e in range(n_exp):
            o = (bb * n_exp + e) * nbp + ii * bpt
            a_e = bb * cap + ((lo_ref[o] >> ROW_SHIFT) << ROW_SHIFT)
            firsts.append(a_e)
            npass = jnp.maximum(npass, (bb * cap + lo_ref[o + bpt] - a_e + WIN - 1) >> WIN_SHIFT)
        return firsts, npass

    def clamp(w):
        return pl.multiple_of(jnp.minimum(w, mtot - WIN), ROW_ALIGN)

    def win_copy(e, row, buf, k):
        return pltpu.make_async_copy(y_ref.at[e, pl.ds(row, WIN)], buf.at[e], sem.at[k])

    starts, npass = tile_windows(b, i)

    @pl.when(step == 0)
    def _():
        for e in range(n_exp):
            win_copy(e, clamp(starts[e]), ybuf.at[0], 0).start()

    @pl.when(step + 1 < pl.num_programs(0) * nt)
    def _():
        wrap = i + 1 == nt
        nxt, _ = tile_windows(jnp.where(wrap, b + 1, b), jnp.where(wrap, 0, i + 1))
        for e in range(n_exp):
            win_copy(e, clamp(nxt[e]), ybuf.at[1 - slot], 1 - slot).start(priority=e % 2)

    acc_ref[...] = jnp.zeros_like(acc_ref)
    c1 = code_ref[0] + 1
    c_hi = (c1 >> CODE_SPLIT_SHIFT).astype(F32).astype(BF16)
    c_lo = (c1 & ((1 << CODE_SPLIT_SHIFT) - 1)).astype(F32).astype(BF16)
    spread = (jnp.dot(c_hi, sel_ref[...], preferred_element_type=F32) * float(1 << CODE_SPLIT_SHIFT)
              + jnp.dot(c_lo, sel_ref[...], preferred_element_type=F32))
    lane = lax.broadcasted_iota(I32, (1, n_exp * WIN), 1)
    lane_e = lane >> WIN_SHIFT
    lane_j = lane & (WIN - 1)

    def per_expert_row(vals):
        row = jnp.zeros_like(lane)
        for e in range(n_exp):
            row = jnp.where(lane_e == e, vals[e], row)
        return row

    def expand(buf, want, rows):
        slot_rank = (per_expert_row([r - base for r in rows]) + lane_j + 1).astype(F32)
        first_rank = (per_expert_row([w - base for w in want]) + 1).astype(F32)
        hit = (spread == slot_rank) & (spread >= first_rank)
        p_mat = jnp.where(hit, 1.0, 0.0).astype(BF16)
        acc_ref[...] += jnp.dot(p_mat, buf[...].reshape(n_exp * WIN, d), preferred_element_type=F32)

    rows0 = [clamp(a) for a in starts]
    for e in range(n_exp):
        win_copy(e, rows0[e], ybuf.at[slot], slot).wait()
    expand(ybuf.at[slot], starts, rows0)

    def later_pass(p, carry):
        want = [a + p * WIN for a in starts]
        rows = [clamp(w) for w in want]
        for e in range(n_exp):
            win_copy(e, rows[e], yextra, 2).start()
        for e in range(n_exp):
            win_copy(e, rows[e], yextra, 2).wait()
        expand(yextra, want, rows)
        return carry

    lax.fori_loop(1, npass, later_pass, 0)

    m = m_ref[0]
    lat_new = x_ref[0] + m[5:6] * acc_ref[...]
    if final:
        out_ref[0] = _rms(lat_new, gfin_ref[...])
    else:
        out_ref[0] = lat_new


def _combine(y, mtot, x, code_t, lo_ext, nbp, m, mrow, gfin, cap, final, tt):
    bn, t, d = x.shape
    n_exp = y.shape[0]
    assert cap >> CODE_SPLIT_SHIFT <= 256
    sel = (jnp.arange(LANES)[:, None] == jnp.arange(n_exp * WIN)[None, :] // WIN).astype(BF16)
    grid_spec = pltpu.PrefetchScalarGridSpec(
        num_scalar_prefetch=1,
        grid=(bn, t // tt),
        in_specs=[pl.BlockSpec(memory_space=pl.ANY),
                  pl.BlockSpec((1, tt, d), lambda b, i, lo: (b, i, 0)),
                  pl.BlockSpec((1, tt, LANES), lambda b, i, lo: (b, i, 0)),
                  pl.BlockSpec((1, 6, d), lambda b, i, lo: (mrow(b), 0, 0)),
                  pl.BlockSpec((1, d), lambda b, i, lo: (0, 0)),
                  pl.BlockSpec((LANES, n_exp * WIN), lambda b, i, lo: (0, 0))],
        out_specs=pl.BlockSpec((1, tt, d), lambda b, i, lo: (b, i, 0)),
        scratch_shapes=[pltpu.VMEM((2, n_exp, WIN, d), BF16), pltpu.VMEM((n_exp, WIN, d), BF16),
                        pltpu.VMEM((tt, d), F32), pltpu.SemaphoreType.DMA((3,))],
    )
    return pl.pallas_call(
        functools.partial(_combine_body, n_exp, nbp, cap, mtot, final),
        grid_spec=grid_spec,
        out_shape=jax.ShapeDtypeStruct((bn, t, d), F32),
        compiler_params=_cparams(("arbitrary", "arbitrary")),
        name="combine",
    )(lo_ext, y, x, code_t, m, gfin, sel)


def _moe(hx_l, aff_l, lat_mid, hx_c, aff_c, ctx_mid, m, layer, wg, wu, wd, final_g, last, tt_l, tt_c):
    bn, t, _ = hx_l.shape
    n_exp = aff_l.shape[1]
    with_ctx = hx_c is not None
    cap_l = CAPACITY_FACTOR * t // n_exp
    code_e_l, code_t_l, lo_l, nbp_l = _route(aff_l, cap_l)
    xl = _dispatch(hx_l, code_e_l, lo_l, nbp_l, cap_l, tt_l)
    xc, cap_c = None, 0
    if with_ctx:
        cap_c = CAPACITY_FACTOR * hx_c.shape[1] // n_exp
        code_e_c, code_t_c, lo_c, nbp_c = _route(aff_c, cap_c)
        xc = _dispatch(hx_c, code_e_c, lo_c, nbp_c, cap_c, tt_c)
    fc = min(1024, wg.shape[-1])
    yl, yc, rows_l = _ffn(xl, xc, wg, wu, wd, layer, cap_l, bn * cap_c, fc)
    lat_new = _combine(yl, rows_l, lat_mid, code_t_l, lo_l, nbp_l, m, lambda b: layer * 8 + b,
                       final_g, cap_l, last, tt_l)
    ctx_new = None
    if with_ctx:
        ctx_new = _combine(yc, bn * cap_c, ctx_mid, code_t_c, lo_c, nbp_c, m,
                           lambda b: layer * 8 + bn, final_g, cap_c, False, tt_c)
    return lat_new, ctx_new


def kernel(x, c, ctx, c_ctx, mod_w, mod_b, norm_mix_g, norm_ffn_g, cv_w1, cv_b1, cv_dw, cv_dwb,
           cv_ln_g, cv_ln_b, cv_w2, cv_b2, hg_w_in, hg_lb_logits, hg_norm_g, hg_w_out,
           router_w, exp_w_gate, exp_w_up, exp_w_down, final_g):
    bn, t, d = x.shape
    tc = ctx.shape[1]
    depth = mod_w.shape[0]
    n_exp = router_w.shape[-1]
    assert bn < 8 and d % HEAD == 0 and n_exp % 4 == 0 and n_exp <= LANES

    cond8 = jnp.concatenate([c, c_ctx[None], jnp.zeros((8 - bn - 1, d), F32)], axis=0)
    m = _mod(cond8, mod_w, mod_b).reshape(depth * 8, 6, d)

    tt_l = min(512, t)
    tt_c = min(256, tc)
    tmoe_l = min(256, t)
    tmoe_c = min(256, tc)
    row = lambda a: a.reshape(1, -1)
    fin = row(final_g)

    lat, cx = x, ctx
    for i in range(depth):
        last = i == depth - 1
        j = i // 2
        lat_row = lambda b, i=i: i * 8 + b
        ctx_row = lambda b, i=i: i * 8 + bn
        gmix, gffn = row(norm_mix_g[i]), row(norm_ffn_g[i])
        rw_pad = jnp.pad(router_w[i], ((0, 0), (0, LANES - n_exp)))
        rwh = rw_pad.astype(BF16)
        rww = jnp.concatenate([rwh, (rw_pad - rwh.astype(F32)).astype(BF16)], axis=1)
        need_ctx = not last
        hx_c = aff_c = ctx_mid = None
        if i % 2 == 0:
            w1 = cv_w1[j].astype(BF16)
            w2 = cv_w2[j].astype(BF16)
            cargs = (cv_dw[j], row(cv_dwb[j]), row(cv_ln_g[j]), row(cv_ln_b[j]), w2, row(cv_b2[j]),
                     gffn, rww, rwh, n_exp)
            u = _glu(lat, m, lat_row, gmix, w1, row(cv_b1[j]), tt_l)
            lat_mid, hx_l, aff_l = _conv(u, lat, m, lat_row, *cargs, tt_l)
            if need_ctx:
                uc = _glu(cx, m, ctx_row, gmix, w1, row(cv_b1[j]), tt_c)
                ctx_mid, hx_c, aff_c = _conv(uc, cx, m, ctx_row, *cargs, tt_c)
        else:
            w_in = hg_w_in[j].astype(BF16)
            w_out = hg_w_out[j].astype(BF16)
            ng = row(jnp.tile(hg_norm_g[j], d // HEAD))
            s_zero = jnp.zeros((bn, d // HEAD, HEAD, HEAD), F32)
            qc, kfc, kbc, lfc, lbc, vc, gc = _proj(cx, m, ctx_row, gmix, w_in, hg_lb_logits, i, tt_c)
            ql, kfl, kbl, lfl, lbl, vl, gl = _proj(lat, m, lat_row, gmix, w_in, hg_lb_logits, i, tt_l)
            tq_c, tq_l = min(256, tc), min(256, t)
            oc_f, oc_b, sc_f, sc_b = _scan(qc, kfc, kbc, vc, lfc, lbc, s_zero, s_zero, tq_c)
            ol_f, ol_b, _, _ = _scan(ql, kfl, kbl, vl, lfl, lbl, sc_f, sc_b, tq_l)
            pargs = (ng, w_out, gffn, rww, rwh, n_exp)
            lat_mid, hx_l, aff_l = _hgpost(ol_f, ol_b, gl, lat, m, lat_row, *pargs, tt_l)
            if need_ctx:
                ctx_mid, hx_c, aff_c = _hgpost(oc_f, oc_b, gc, cx, m, ctx_row, *pargs, tt_c)
        lat, cx_new = _moe(hx_l, aff_l, lat_mid, hx_c, aff_c, ctx_mid, m, i, exp_w_gate, exp_w_up,
                           exp_w_down, fin, last, tmoe_l, tmoe_c)
        if need_ctx:
            cx = cx_new
    return lat
```
